```python
import jax, jax.numpy as jnp
from jax import lax
import numpy as np

D_MODEL = 2048
BATCH = 2
SEQ = 4096
DEPTH = 4

CHUNK = 64
Q_BLOCK = 128
N_MIXERS = 2
CONV_WIDTH = 31
N_HEADS = 16
QK_NOPE_DIM = 128
QK_ROPE_DIM = 64
V_HEAD_DIM = 128
Q_LORA_RANK = 512
KV_LORA_RANK = 512
D_FF = 4 * D_MODEL
ROPE_THETA = 10000.0
NORM_EPS = 1e-6
LN_EPS = 1e-5
N_CONV_LAYERS = (DEPTH + 1) // 2
N_MLA_LAYERS = DEPTH // 2

kernel_name = "hybrid_conformer_conv_mla_sqrelu_trunk"


def rms_norm(x, g):
    xf = x.astype(jnp.float32)
    y = xf * lax.rsqrt(jnp.mean(xf * xf, axis=-1, keepdims=True) + NORM_EPS)
    return (y * g.astype(jnp.float32)).astype(x.dtype)


def layer_norm(x, g, b):
    xf = x.astype(jnp.float32)
    mu = jnp.mean(xf, axis=-1, keepdims=True)
    xc = xf - mu
    var = jnp.mean(xc * xc, axis=-1, keepdims=True)
    y = xc * lax.rsqrt(var + LN_EPS) * g.astype(jnp.float32) + b.astype(jnp.float32)
    return y.astype(x.dtype)


def conv_module(h, w_pw1, b_pw1, w_dw, b_dw, ln_g, ln_b, w_pw2, b_pw2):
    u = h @ w_pw1 + b_pw1
    a, gate = jnp.split(u, 2, axis=-1)
    u = a * jax.nn.sigmoid(gate)
    u = lax.conv_general_dilated(
        u, w_dw[:, None, :].astype(u.dtype), window_strides=(1,),
        padding=[(CONV_WIDTH - 1, 0)],
        dimension_numbers=("NWC", "WIO", "NWC"),
        feature_group_count=D_MODEL) + b_dw
    u = jax.nn.silu(layer_norm(u, ln_g, ln_b))
    return u @ w_pw2 + b_pw2


def rope_tables(positions):
    inv_freq = ROPE_THETA ** (-jnp.arange(0, QK_ROPE_DIM, 2, dtype=jnp.float32) / QK_ROPE_DIM)
    ang = positions.astype(jnp.float32)[..., None] * inv_freq
    return jnp.cos(ang), jnp.sin(ang)


def apply_rope(x, cos, sin):
    xf = x.astype(jnp.float32)
    x1, x2 = jnp.split(xf, 2, axis=-1)
    out = jnp.concatenate([x1 * cos - x2 * sin, x2 * cos + x1 * sin], axis=-1)
    return out.astype(x.dtype)


def mla(h, cos, sin, w_in, q_norm_g, kv_norm_g, w_q_up, w_kv_up, w_o):
    B, S, _ = h.shape
    down = h @ w_in
    c_q, c_kv, k_pe = jnp.split(down, [Q_LORA_RANK, Q_LORA_RANK + KV_LORA_RANK], axis=-1)
    q = (rms_norm(c_q, q_norm_g) @ w_q_up).reshape(B, S, N_HEADS, QK_NOPE_DIM + QK_ROPE_DIM)
    q_nope, q_pe = jnp.split(q, [QK_NOPE_DIM], axis=-1)
    q_pe = apply_rope(q_pe, cos[:, :, None, :], sin[:, :, None, :])
    k_pe = apply_rope(k_pe, cos, sin)
    kv = (rms_norm(c_kv, kv_norm_g) @ w_kv_up).reshape(B, S, N_HEADS, QK_NOPE_DIM + V_HEAD_DIM)
    k_nope, v = jnp.split(kv, [QK_NOPE_DIM], axis=-1)

    n_blk = S // Q_BLOCK
    qn_blocks = q_nope.reshape(B, n_blk, Q_BLOCK, N_HEADS, QK_NOPE_DIM).transpose(1, 0, 2, 3, 4)
    qp_blocks = q_pe.reshape(B, n_blk, Q_BLOCK, N_HEADS, QK_ROPE_DIM).transpose(1, 0, 2, 3, 4)
    k_chunk = jnp.arange(S) // CHUNK
    scale = (QK_NOPE_DIM + QK_ROPE_DIM) ** -0.5

    def attend_block(args):
        blk, qn, qp = args
        s = (jnp.einsum('bqhd,bkhd->bhqk', qn, k_nope)
             + jnp.einsum('bqhr,bkr->bhqk', qp, k_pe)).astype(jnp.float32) * scale
        q_chunk = (blk * Q_BLOCK + jnp.arange(Q_BLOCK)) // CHUNK
        mask = k_chunk[None, :] <= q_chunk[:, None]
        s = jnp.where(mask[None, None], s, -jnp.inf)
        p = jax.nn.softmax(s, axis=-1).astype(v.dtype)
        return jnp.einsum('bhqk,bkhd->bqhd', p, v)

    o = lax.map(attend_block, (jnp.arange(n_blk), qn_blocks, qp_blocks))
    o = o.transpose(1, 0, 2, 3, 4).reshape(B, S, N_HEADS * V_HEAD_DIM)
    return o @ w_o


def sq_relu_mlp(h, w1, w2):
    return jnp.square(jax.nn.relu(h @ w1)) @ w2


def setup_inputs(seed: int = 0) -> dict:
    key = jax.random.key(seed)
    ks = iter(jax.random.split(key, 32))
    f32 = jnp.float32

    def dense(shape, fan_in):
        return jax.random.normal(next(ks), shape, f32) * (fan_in ** -0.5)

    def gain(shape):
        return 1.0 + 0.02 * jax.random.normal(next(ks), shape, f32)

    def bias(shape):
        return 0.02 * jax.random.normal(next(ks), shape, f32)

    Nc, Nm, D = N_CONV_LAYERS, N_MLA_LAYERS, D_MODEL
    x = jax.random.normal(next(ks), (BATCH, SEQ, D), f32)
    positions = jnp.broadcast_to(jnp.arange(SEQ, dtype=jnp.int32)[None, :], (BATCH, SEQ))
    return {
        "x": x,
        "positions": positions,
        "norm_mixer_g": gain((DEPTH, D)),
        "norm_mlp_g": gain((DEPTH, D)),
        "conv_w_pw1": dense((Nc, D, 2 * D), D),
        "conv_b_pw1": bias((Nc, 2 * D)),
        "conv_w_dw": dense((Nc, CONV_WIDTH, D), CONV_WIDTH),
        "conv_b_dw": bias((Nc, D)),
        "conv_ln_g": gain((Nc, D)),
        "conv_ln_b": bias((Nc, D)),
        "conv_w_pw2": dense((Nc, D, D), D),
        "conv_b_pw2": bias((Nc, D)),
        "mla_w_in": dense((Nm, D, Q_LORA_RANK + KV_LORA_RANK + QK_ROPE_DIM), D),
        "mla_q_norm_g": gain((Nm, Q_LORA_RANK)),
        "mla_kv_norm_g": gain((Nm, KV_LORA_RANK)),
        "mla_w_q_up": dense((Nm, Q_LORA_RANK, N_HEADS * (QK_NOPE_DIM + QK_ROPE_DIM)), Q_LORA_RANK),
        "mla_w_kv_up": dense((Nm, KV_LORA_RANK, N_HEADS * (QK_NOPE_DIM + V_HEAD_DIM)), KV_LORA_RANK),
        "mla_w_o": dense((Nm, N_HEADS * V_HEAD_DIM, D), N_HEADS * V_HEAD_DIM),
        "mlp_w1": dense((DEPTH, D, D_FF), D),
        "mlp_w2": dense((DEPTH, D_FF, D), D_FF),
        "final_norm_g": gain((D,)),
    }


def reference(x, positions, norm_mixer_g, norm_mlp_g,
              conv_w_pw1, conv_b_pw1, conv_w_dw, conv_b_dw, conv_ln_g, conv_ln_b,
              conv_w_pw2, conv_b_pw2,
              mla_w_in, mla_q_norm_g, mla_kv_norm_g, mla_w_q_up, mla_w_kv_up, mla_w_o,
              mlp_w1, mlp_w2, final_norm_g):
    cos, sin = rope_tables(positions)
    for layer in range(DEPTH):
        j = layer // N_MIXERS
        h = rms_norm(x, norm_mixer_g[layer])
        if layer % N_MIXERS == 0:
            x = x + conv_module(h, conv_w_pw1[j], conv_b_pw1[j], conv_w_dw[j], conv_b_dw[j],
                                conv_ln_g[j], conv_ln_b[j], conv_w_pw2[j], conv_b_pw2[j])
        else:
            x = x + mla(h, cos, sin, mla_w_in[j], mla_q_norm_g[j], mla_kv_norm_g[j],
                        mla_w_q_up[j], mla_w_kv_up[j], mla_w_o[j])
        h = rms_norm(x, norm_mlp_g[layer])
        x = x + sq_relu_mlp(h, mlp_w1[layer], mlp_w2[layer])
    return rms_norm(x, final_norm_g)
```

```python
import functools
import math

import jax
import jax.numpy as jnp
from jax import lax
from jax.experimental import pallas as pl
from jax.experimental.pallas import tpu as pltpu

D_MODEL = 2048
DEPTH = 4
CHUNK = 64
N_MIXERS = 2
CONV_WIDTH = 31
N_HEADS = 16
QK_NOPE_DIM = 128
QK_ROPE_DIM = 64
V_HEAD_DIM = 128
Q_LORA_RANK = 512
KV_LORA_RANK = 512
D_FF = 4 * D_MODEL
ROPE_THETA = 10000.0
NORM_EPS = 1e-6
LN_EPS = 1e-5

LANES = 128
QK_PAD_DIM = 2 * LANES
HALO = 32
VMEM_LIMIT = 56 * 1024 * 1024

F32 = jnp.float32
BF16 = jnp.bfloat16


def _rms_rows(x, g, eps):
    ms = jnp.mean(x * x, axis=-1, keepdims=True)
    return x * lax.rsqrt(ms + eps) * g


def _rmsnorm_into(x_ref, g_ref, out_ref, rows, chunk=256):
    g = g_ref[...]

    def body(c, carry):
        r = pl.multiple_of(c * chunk, chunk)
        x = x_ref[pl.ds(r, chunk), :]
        out_ref[pl.ds(r, chunk), :] = _rms_rows(x, g, NORM_EPS).astype(out_ref.dtype)
        return carry

    lax.fori_loop(0, rows // chunk, body, 0)


def _pw1_glu_kernel(x_ref, g_ref, wa_ref, wg_ref, ba_ref, bg_ref, o_ref, xn_ref, *, tm):
    @pl.when(pl.program_id(1) == 0)
    def _():
        _rmsnorm_into(x_ref, g_ref, xn_ref, tm)

    xn = xn_ref[...]
    a = jnp.dot(xn, wa_ref[...], preferred_element_type=F32) + ba_ref[...]
    gate = jnp.dot(xn, wg_ref[...], preferred_element_type=F32) + bg_ref[...]
    o_ref[...] = a * jax.nn.sigmoid(gate)


def _pw1_glu(x, g, w_bf16, b, *, tm=1024, tn=512):
    m, d = x.shape
    nj = d // tn
    return pl.pallas_call(
        functools.partial(_pw1_glu_kernel, tm=tm),
        grid=(m // tm, nj),
        in_specs=[
            pl.BlockSpec((tm, d), lambda i, j: (i, 0)),
            pl.BlockSpec((1, d), lambda i, j: (0, 0)),
            pl.BlockSpec((d, tn), lambda i, j: (0, j)),
            pl.BlockSpec((d, tn), lambda i, j: (0, j + nj)),
            pl.BlockSpec((1, tn), lambda i, j: (0, j)),
            pl.BlockSpec((1, tn), lambda i, j: (0, j + nj)),
        ],
        out_specs=pl.BlockSpec((tm, tn), lambda i, j: (i, j)),
        out_shape=jax.ShapeDtypeStruct((m, d), F32),
        scratch_shapes=[pltpu.VMEM((tm, d), BF16)],
        compiler_params=pltpu.CompilerParams(
            dimension_semantics=("parallel", "arbitrary"),
            vmem_limit_bytes=VMEM_LIMIT),
        name="pw1_glu",
    )(x, g, w_bf16, w_bf16, b, b)


def _dwconv_ln_kernel(u_ref, halo_ref, w_ref, b_ref, lg_ref, lb_ref, o_ref, buf_ref, conv_ref,
                      *, tc, tiles_per_seq):
    d = u_ref.shape[1]
    first = (pl.program_id(0) % tiles_per_seq) == 0
    halo = halo_ref[...]
    buf_ref[0:HALO, :] = jnp.where(first, jnp.zeros_like(halo), halo)
    buf_ref[HALO:, :] = u_ref[...]

    rows = 32
    off = HALO - (CONV_WIDTH - 1)
    for r in range(0, tc, rows):
        for c in range(0, d, LANES):
            acc = jnp.zeros((rows, LANES), F32)
            for k in range(CONV_WIDTH):
                acc = acc + buf_ref[r + off + k:r + off + k + rows, c:c + LANES] * w_ref[k:k + 1, c:c + LANES]
            conv_ref[r:r + rows, c:c + LANES] = acc

    y = conv_ref[...] + b_ref[...]
    mu = jnp.mean(y, axis=-1, keepdims=True)
    yc = y - mu
    var = jnp.mean(yc * yc, axis=-1, keepdims=True)
    z = yc * lax.rsqrt(var + LN_EPS) * lg_ref[...] + lb_ref[...]
    o_ref[...] = (z * jax.nn.sigmoid(z)).astype(o_ref.dtype)


def _dwconv_ln(u, w_dw, b_dw, ln_g, ln_b, *, seq, tc=256):
    m, d = u.shape
    tiles_per_seq = seq // tc
    hb = tc // HALO
    return pl.pallas_call(
        functools.partial(_dwconv_ln_kernel, tc=tc, tiles_per_seq=tiles_per_seq),
        grid=(m // tc,),
        in_specs=[
            pl.BlockSpec((tc, d), lambda i: (i, 0)),
            pl.BlockSpec((HALO, d), lambda i: (jnp.maximum(i * hb - 1, 0), 0)),
            pl.BlockSpec((CONV_WIDTH, d), lambda i: (0, 0)),
            pl.BlockSpec((1, d), lambda i: (0, 0)),
            pl.BlockSpec((1, d), lambda i: (0, 0)),
            pl.BlockSpec((1, d), lambda i: (0, 0)),
        ],
        out_specs=pl.BlockSpec((tc, d), lambda i: (i, 0)),
        out_shape=jax.ShapeDtypeStruct((m, d), BF16),
        scratch_shapes=[pltpu.VMEM((tc + HALO, d), F32), pltpu.VMEM((tc, d), F32)],
        compiler_params=pltpu.CompilerParams(
            dimension_semantics=("parallel",),
            vmem_limit_bytes=VMEM_LIMIT),
        name="dwconv_ln",
    )(u, u, w_dw, b_dw, ln_g, ln_b)


def _linear_residual_kernel(*refs, has_bias):
    if has_bias:
        x_ref, a_ref, w_ref, b_ref, o_ref = refs
    else:
        x_ref, a_ref, w_ref, o_ref = refs
    y = jnp.dot(a_ref[...], w_ref[...], preferred_element_type=F32)
    if has_bias:
        y = y + b_ref[...]
    o_ref[...] = x_ref[...] + y


def _linear_residual(x, a, w_bf16, b=None, *, tm=1024, tn=1024):
    m, n = x.shape
    k = a.shape[1]
    in_specs = [
        pl.BlockSpec((tm, tn), lambda i, j: (i, j)),
        pl.BlockSpec((tm, k), lambda i, j: (i, 0)),
        pl.BlockSpec((k, tn), lambda i, j: (0, j)),
    ]
    args = [x, a, w_bf16]
    if b is not None:
        in_specs.append(pl.BlockSpec((1, tn), lambda i, j: (0, j)))
        args.append(b)
    return pl.pallas_call(
        functools.partial(_linear_residual_kernel, has_bias=b is not None),
        grid=(m // tm, n // tn),
        in_specs=in_specs,
        out_specs=pl.BlockSpec((tm, tn), lambda i, j: (i, j)),
        out_shape=jax.ShapeDtypeStruct((m, n), F32),
        compiler_params=pltpu.CompilerParams(
            dimension_semantics=("parallel", "arbitrary"),
            vmem_limit_bytes=VMEM_LIMIT),
        name="linear_residual",
    )(*args)


def _mlp_kernel(*refs, tm, final_norm):
    if final_norm:
        x_ref, g_ref, w1_ref, w2_ref, fg_ref, o_ref, xn_ref = refs
    else:
        x_ref, g_ref, w1_ref, w2_ref, o_ref, xn_ref = refs
    f = pl.program_id(1)

    @pl.when(f == 0)
    def _():
        _rmsnorm_into(x_ref, g_ref, xn_ref, tm)
        o_ref[...] = x_ref[...]

    h = jnp.dot(xn_ref[...], w1_ref[...], preferred_element_type=F32)
    h = jnp.square(jnp.maximum(h, 0.0)).astype(BF16)
    o_ref[...] += jnp.dot(h, w2_ref[...], preferred_element_type=F32)

    if final_norm:
        @pl.when(f == pl.num_programs(1) - 1)
        def _():
            _rmsnorm_into(o_ref, fg_ref, o_ref, tm)


def _mlp(x, g, w1_bf16, w2_bf16, final_g=None, *, tm=1024, tf=512):
    m, d = x.shape
    dff = w1_bf16.shape[1]
    final_norm = final_g is not None
    in_specs = [
        pl.BlockSpec((tm, d), lambda i, f: (i, 0)),
        pl.BlockSpec((1, d), lambda i, f: (0, 0)),
        pl.BlockSpec((d, tf), lambda i, f: (0, f)),
        pl.BlockSpec((tf, d), lambda i, f: (f, 0)),
    ]
    args = [x, g, w1_bf16, w2_bf16]
    if final_norm:
        in_specs.append(pl.BlockSpec((1, d), lambda i, f: (0, 0)))
        args.append(final_g)
    return pl.pallas_call(
        functools.partial(_mlp_kernel, tm=tm, final_norm=final_norm),
        grid=(m // tm, dff // tf),
        in_specs=in_specs,
        out_specs=pl.BlockSpec((tm, d), lambda i, f: (i, 0)),
        out_shape=jax.ShapeDtypeStruct((m, d), F32),
        scratch_shapes=[pltpu.VMEM((tm, d), BF16)],
        compiler_params=pltpu.CompilerParams(
            dimension_semantics=("parallel", "arbitrary"),
            vmem_limit_bytes=VMEM_LIMIT),
        name="mlp",
    )(*args)


def _rope_table_kernel(pos_ref, invf_ref, cos_ref, sin_ref):
    ang = pos_ref[...].astype(F32) * invf_ref[...]
    cos_ref[...] = jnp.cos(ang)
    sin_ref[...] = jnp.sin(ang)


def _rope_tables(positions):
    half = QK_ROPE_DIM // 2
    m = positions.size
    rows = m * half // LANES
    pos_dense = jnp.repeat(positions.reshape(m), half).reshape(rows, LANES)
    inv_freq = ROPE_THETA ** (-jnp.arange(0, QK_ROPE_DIM, 2, dtype=F32) / QK_ROPE_DIM)
    invf_dense = jnp.tile(inv_freq, LANES // half).reshape(1, LANES)
    tr = 512
    cos_d, sin_d = pl.pallas_call(
        _rope_table_kernel,
        grid=(rows // tr,),
        in_specs=[pl.BlockSpec((tr, LANES), lambda i: (i, 0)),
                  pl.BlockSpec((1, LANES), lambda i: (0, 0))],
        out_specs=[pl.BlockSpec((tr, LANES), lambda i: (i, 0)),
                   pl.BlockSpec((tr, LANES), lambda i: (i, 0))],
        out_shape=[jax.ShapeDtypeStruct((rows, LANES), F32)] * 2,
        compiler_params=pltpu.CompilerParams(dimension_semantics=("parallel",)),
        name="rope_tables",
    )(pos_dense, invf_dense)
    cos = cos_d.reshape(m, half)
    sin = sin_d.reshape(m, half)
    zeros = jnp.zeros((m, LANES - QK_ROPE_DIM), F32)
    return (jnp.concatenate([cos, cos, zeros], axis=1),
            jnp.concatenate([sin, sin, zeros], axis=1))


def _mla_proj_kernel(x_ref, g_ref, cos_ref, sin_ref, win_ref, qg_ref, kvg_ref,
                     wq_ref, wqs_ref, wkn_ref, wv_ref,
                     q_ref, kn_ref, kpe_ref, v_ref, *, q_scale):
    xn = _rms_rows(x_ref[...], g_ref[...], NORM_EPS).astype(BF16)
    down = jnp.dot(xn, win_ref[...], preferred_element_type=F32)
    c_q = down[:, :Q_LORA_RANK]
    c_kv = down[:, Q_LORA_RANK:Q_LORA_RANK + KV_LORA_RANK]
    base = Q_LORA_RANK + KV_LORA_RANK
    cos = cos_ref[...]
    sin = sin_ref[...]
    kpe = down[:, base:base + LANES] * cos + down[:, base + LANES:base + 2 * LANES] * sin
    kpe_ref[...] = kpe.astype(kpe_ref.dtype)

    cqn = _rms_rows(c_q, qg_ref[...], NORM_EPS).astype(BF16)
    ckvn = _rms_rows(c_kv, kvg_ref[...], NORM_EPS).astype(BF16)

    q = jnp.dot(cqn, wq_ref[...], preferred_element_type=F32)
    qs = jnp.dot(cqn, wqs_ref[...], preferred_element_type=F32)
    for h in range(N_HEADS):
        lo = h * QK_PAD_DIM
        q_ref[:, lo:lo + LANES] = (q[:, lo:lo + LANES] * q_scale).astype(q_ref.dtype)
        rot = q[:, lo + LANES:lo + 2 * LANES] * cos + qs[:, h * LANES:(h + 1) * LANES] * sin
        q_ref[:, lo + LANES:lo + 2 * LANES] = (rot * q_scale).astype(q_ref.dtype)

    kn_ref[...] = jnp.dot(ckvn, wkn_ref[...], preferred_element_type=F32).astype(kn_ref.dtype)
    v_ref[...] = jnp.dot(ckvn, wv_ref[...], preferred_element_type=F32).astype(v_ref.dtype)


def _mla_proj(x, g, cos_t, sin_t, w_in_p, q_g, kv_g, wq_p, wqs_p, wkn_p, wv_p, *, q_scale, tm=256):
    m, d = x.shape

    def resident(a):
        return pl.BlockSpec(a.shape, lambda i: (0, 0))

    def rows(width):
        return pl.BlockSpec((tm, width), lambda i: (i, 0))

    return pl.pallas_call(
        functools.partial(_mla_proj_kernel, q_scale=q_scale),
        grid=(m // tm,),
        in_specs=[rows(d), resident(g), rows(LANES), rows(LANES), resident(w_in_p),
                  resident(q_g), resident(kv_g), resident(wq_p), resident(wqs_p),
                  resident(wkn_p), resident(wv_p)],
        out_specs=[rows(N_HEADS * QK_PAD_DIM), rows(N_HEADS * QK_NOPE_DIM), rows(LANES),
                   rows(N_HEADS * V_HEAD_DIM)],
        out_shape=[jax.ShapeDtypeStruct((m, N_HEADS * QK_PAD_DIM), BF16),
                   jax.ShapeDtypeStruct((m, N_HEADS * QK_NOPE_DIM), BF16),
                   jax.ShapeDtypeStruct((m, LANES), BF16),
                   jax.ShapeDtypeStruct((m, N_HEADS * V_HEAD_DIM), BF16)],
        compiler_params=pltpu.CompilerParams(
            dimension_semantics=("parallel",),
            vmem_limit_bytes=VMEM_LIMIT),
        name="mla_proj",
    )(x, g, cos_t, sin_t, w_in_p, q_g, kv_g, wq_p, wqs_p, wkn_p, wv_p)


def _attn_kernel(q_ref, kn_ref, kpe_ref, v_ref, o_ref, *, tq, tk):
    qi = pl.program_id(2)
    q = q_ref[...]

    def step(kb, carry, masked):
        m_prev, l_prev, acc = carry
        r = pl.multiple_of(kb * tk, tk)
        k = jnp.concatenate([kn_ref[pl.ds(r, tk), :], kpe_ref[pl.ds(r, tk), :]], axis=1)
        s = lax.dot_general(q, k, (((1,), (1,)), ((), ())), preferred_element_type=F32)
        if masked:
            q_chunk = lax.broadcasted_iota(jnp.int32, (tq, tk), 0) // CHUNK
            k_chunk = lax.broadcasted_iota(jnp.int32, (tq, tk), 1) // CHUNK
            s = jnp.where(k_chunk <= q_chunk, s, -jnp.inf)
        m_new = jnp.maximum(m_prev, jnp.max(s, axis=-1, keepdims=True))
        alpha = jnp.exp2(m_prev - m_new)
        p = jnp.exp2(s - m_new)
        l_new = alpha * l_prev + jnp.sum(p, axis=-1, keepdims=True)
        pv = jnp.dot(p.astype(BF16), v_ref[pl.ds(r, tk), :], preferred_element_type=F32)
        return m_new, l_new, alpha * acc + pv

    init = (jnp.full((tq, 1), -jnp.inf, F32), jnp.zeros((tq, 1), F32),
            jnp.zeros((tq, V_HEAD_DIM), F32))
    carry = lax.fori_loop(0, qi, functools.partial(step, masked=False), init)
    _, l_fin, acc = step(qi, carry, masked=True)
    o_ref[...] = (acc / l_fin).astype(o_ref.dtype)


def _attention(q, kn, kpe, v, *, batch, seq, tq=512):
    tk = tq
    nq = seq // tq
    return pl.pallas_call(
        functools.partial(_attn_kernel, tq=tq, tk=tk),
        grid=(batch, N_HEADS, nq),
        in_specs=[
            pl.BlockSpec((tq, QK_PAD_DIM), lambda b, h, i: (b * nq + i, h)),
            pl.BlockSpec((seq, QK_NOPE_DIM), lambda b, h, i: (b, h)),
            pl.BlockSpec((seq, LANES), lambda b, h, i: (b, 0)),
            pl.BlockSpec((seq, V_HEAD_DIM), lambda b, h, i: (b, h)),
        ],
        out_specs=pl.BlockSpec((tq, V_HEAD_DIM), lambda b, h, i: (b * nq + i, h)),
        out_shape=jax.ShapeDtypeStruct((batch * seq, N_HEADS * V_HEAD_DIM), BF16),
        compiler_params=pltpu.CompilerParams(
            dimension_semantics=("parallel", "parallel", "arbitrary"),
            vmem_limit_bytes=VMEM_LIMIT),
        name="mla_attention",
    )(q, kn, kpe, v)


def _prep_mla_weights(w_in, w_q_up, w_kv_up):
    half = QK_ROPE_DIM // 2
    base = Q_LORA_RANK + KV_LORA_RANK
    pad = jnp.zeros((w_in.shape[0], LANES - QK_ROPE_DIM), w_in.dtype)
    k_pe = w_in[:, base:]
    w_in_p = jnp.concatenate(
        [w_in[:, :base], k_pe, pad, -k_pe[:, half:], k_pe[:, :half], pad], axis=1)

    r = w_q_up.shape[0]
    wq3 = w_q_up.reshape(r, N_HEADS, QK_NOPE_DIM + QK_ROPE_DIM)
    nope, pe = wq3[:, :, :QK_NOPE_DIM], wq3[:, :, QK_NOPE_DIM:]
    hpad = jnp.zeros((r, N_HEADS, LANES - QK_ROPE_DIM), w_q_up.dtype)
    wq_p = jnp.concatenate([nope, pe, hpad], axis=2).reshape(r, N_HEADS * QK_PAD_DIM)
    wqs_p = jnp.concatenate([-pe[:, :, half:], pe[:, :, :half], hpad], axis=2).reshape(r, N_HEADS * LANES)

    rk = w_kv_up.shape[0]
    wkv3 = w_kv_up.reshape(rk, N_HEADS, QK_NOPE_DIM + V_HEAD_DIM)
    wkn_p = wkv3[:, :, :QK_NOPE_DIM].reshape(rk, N_HEADS * QK_NOPE_DIM)
    wv_p = wkv3[:, :, QK_NOPE_DIM:].reshape(rk, N_HEADS * V_HEAD_DIM)
    return tuple(w.astype(BF16) for w in (w_in_p, wq_p, wqs_p, wkn_p, wv_p))


def kernel(x, positions, norm_mixer_g, norm_mlp_g, conv_w_pw1, conv_b_pw1, conv_w_dw, conv_b_dw, conv_ln_g, conv_ln_b, conv_w_pw2, conv_b_pw2, mla_w_in, mla_q_norm_g, mla_kv_norm_g, mla_w_q_up, mla_w_kv_up, mla_w_o, mlp_w1, mlp_w2, final_norm_g):
    batch, seq, d = x.shape
    m = batch * seq
    xs = x.reshape(m, d)

    def row(v):
        return v.reshape(1, -1)

    cos_t, sin_t = _rope_tables(positions)
    q_scale = (QK_NOPE_DIM + QK_ROPE_DIM) ** -0.5 * math.log2(math.e)

    for layer in range(DEPTH):
        j = layer // N_MIXERS
        g_mix = row(norm_mixer_g[layer])
        if layer % N_MIXERS == 0:
            u = _pw1_glu(xs, g_mix, conv_w_pw1[j].astype(BF16), row(conv_b_pw1[j]))
            a = _dwconv_ln(u, conv_w_dw[j], row(conv_b_dw[j]), row(conv_ln_g[j]),
                           row(conv_ln_b[j]), seq=seq)
            xs = _linear_residual(xs, a, conv_w_pw2[j].astype(BF16), row(conv_b_pw2[j]))
        else:
            w_in_p, wq_p, wqs_p, wkn_p, wv_p = _prep_mla_weights(
                mla_w_in[j], mla_w_q_up[j], mla_w_kv_up[j])
            q, kn, kpe, v = _mla_proj(xs, g_mix, cos_t, sin_t, w_in_p,
                                      row(mla_q_norm_g[j]), row(mla_kv_norm_g[j]),
                                      wq_p, wqs_p, wkn_p, wv_p, q_scale=q_scale)
            o = _attention(q, kn, kpe, v, batch=batch, seq=seq)
            xs = _linear_residual(xs, o, mla_w_o[j].astype(BF16))
        final_g = row(final_norm_g) if layer == DEPTH - 1 else None
        xs = _mlp(xs, row(norm_mlp_g[layer]), mlp_w1[layer].astype(BF16),
                  mlp_w2[layer].astype(BF16), final_g)
    return xs.reshape(batch, seq, d)
```

```python
import functools
import math

import jax
import jax.numpy as jnp
from jax import lax
from jax.experimental import pallas as pl
from jax.experimental.pallas import tpu as pltpu

D_MODEL = 2048
DEPTH = 4
CHUNK = 64
N_MIXERS = 2
CONV_WIDTH = 31
N_HEADS = 16
QK_NOPE_DIM = 128
QK_ROPE_DIM = 64
V_HEAD_DIM = 128
Q_LORA_RANK = 512
KV_LORA_RANK = 512
D_FF = 4 * D_MODEL
ROPE_THETA = 10000.0
NORM_EPS = 1e-6
LN_EPS = 1e-5

SUBLANES = 8
LANES = 128
QK_HEAD_DIM = QK_NOPE_DIM + QK_ROPE_DIM
QK_PAD_DIM = 2 * LANES
HALO = 32
CONV_ROWS = 32
CONV_COLS = 512
VMEM_LIMIT = 56 * 1024 * 1024

F32 = jnp.float32
BF16 = jnp.bfloat16

_NT = (((1,), (1,)), ((), ()))


def _rms_rows(x, g, eps):
    ms = jnp.mean(x * x, axis=-1, keepdims=True)
    return x * lax.rsqrt(ms + eps) * g


def _rmsnorm_into(x_ref, g_ref, out_ref, rows, chunk=256):
    g = g_ref[...]

    def body(c, carry):
        r = pl.multiple_of(c * chunk, chunk)
        x = x_ref[pl.ds(r, chunk), :]
        out_ref[pl.ds(r, chunk), :] = _rms_rows(x, g, NORM_EPS).astype(out_ref.dtype)
        return carry

    lax.fori_loop(0, rows // chunk, body, 0)


def _pw1_glu_kernel(x_ref, g_ref, wa_ref, wg_ref, ba_ref, bg_ref, o_ref, xn_ref, *, tm):
    @pl.when(pl.program_id(1) == 0)
    def _():
        _rmsnorm_into(x_ref, g_ref, xn_ref, tm)

    xn = xn_ref[...]
    a = jnp.dot(xn, wa_ref[...], preferred_element_type=F32) + ba_ref[...]
    gate = jnp.dot(xn, wg_ref[...], preferred_element_type=F32) + bg_ref[...]
    o_ref[...] = a * jax.nn.sigmoid(gate)


def _pw1_glu(x, g, w_all, layer, b, *, tm=1024, tn=512):
    m, d = x.shape
    nj = d // tn
    return pl.pallas_call(
        functools.partial(_pw1_glu_kernel, tm=tm),
        grid=(m // tm, nj),
        in_specs=[
            pl.BlockSpec((tm, d), lambda i, j: (i, 0)),
            pl.BlockSpec((1, d), lambda i, j: (0, 0)),
            pl.BlockSpec((None, d, tn), lambda i, j: (layer, 0, j)),
            pl.BlockSpec((None, d, tn), lambda i, j: (layer, 0, j + nj)),
            pl.BlockSpec((1, tn), lambda i, j: (0, j)),
            pl.BlockSpec((1, tn), lambda i, j: (0, j + nj)),
        ],
        out_specs=pl.BlockSpec((tm, tn), lambda i, j: (i, j)),
        out_shape=jax.ShapeDtypeStruct((m, d), F32),
        scratch_shapes=[pltpu.VMEM((tm, d), BF16)],
        compiler_params=pltpu.CompilerParams(
            dimension_semantics=("parallel", "arbitrary"),
            vmem_limit_bytes=VMEM_LIMIT),
        name="pw1_glu",
    )(x, g, w_all, w_all, b, b)


def _dwconv_ln_kernel(u_ref, halo_ref, w_ref, b_ref, lg_ref, lb_ref, o_ref, sh_ref, conv_ref,
                      *, tc, tiles_per_seq):
    d = u_ref.shape[1]
    first = (pl.program_id(0) % tiles_per_seq) == 0
    halo = halo_ref[...]
    sh_ref[0, 0:HALO, :] = jnp.where(first, jnp.zeros_like(halo), halo)
    sh_ref[0, HALO:, :] = u_ref[...]
    n = tc + HALO - SUBLANES
    for s in range(1, SUBLANES):
        sh_ref[s, 0:n, :] = sh_ref[0, s:s + n, :]

    off = HALO - (CONV_WIDTH - 1)

    def row_body(ri, carry):
        r = pl.multiple_of(ri * CONV_ROWS, CONV_ROWS)
        groups = range(0, CONV_ROWS, SUBLANES)
        for c in range(0, d, CONV_COLS):
            cs = slice(c, c + CONV_COLS)
            bias = b_ref[:, cs]
            accs = [bias for _ in groups]
            for k in range(CONV_WIDTH):
                a, s = divmod(k + off, SUBLANES)
                wk = w_ref[k, :, cs]
                accs = [acc + sh_ref[s, pl.ds(r + SUBLANES * a + g, SUBLANES), cs] * wk
                        for acc, g in zip(accs, groups)]
            for acc, g in zip(accs, groups):
                conv_ref[pl.ds(r + g, SUBLANES), cs] = acc
        return carry

    lax.fori_loop(0, tc // CONV_ROWS, row_body, 0)

    y = conv_ref[...]
    mu = jnp.mean(y, axis=-1, keepdims=True)
    yc = y - mu
    var = jnp.mean(yc * yc, axis=-1, keepdims=True)
    z = yc * lax.rsqrt(var + LN_EPS) * lg_ref[...] + lb_ref[...]
    o_ref[...] = (z * jax.nn.sigmoid(z)).astype(o_ref.dtype)


def _dwconv_ln(u, w_dw, b_dw, ln_g, ln_b, *, seq, tc=256):
    m, d = u.shape
    halo_blocks = tc // HALO
    w_rep = jnp.broadcast_to(w_dw[:, None, :], (CONV_WIDTH, SUBLANES, d))
    b_rep = jnp.broadcast_to(b_dw, (SUBLANES, d))
    return pl.pallas_call(
        functools.partial(_dwconv_ln_kernel, tc=tc, tiles_per_seq=seq // tc),
        grid=(m // tc,),
        in_specs=[
            pl.BlockSpec((tc, d), lambda i: (i, 0)),
            pl.BlockSpec((HALO, d), lambda i: (jnp.maximum(i * halo_blocks - 1, 0), 0)),
            pl.BlockSpec((CONV_WIDTH, SUBLANES, d), lambda i: (0, 0, 0)),
            pl.BlockSpec((SUBLANES, d), lambda i: (0, 0)),
            pl.BlockSpec((1, d), lambda i: (0, 0)),
            pl.BlockSpec((1, d), lambda i: (0, 0)),
        ],
        out_specs=pl.BlockSpec((tc, d), lambda i: (i, 0)),
        out_shape=jax.ShapeDtypeStruct((m, d), BF16),
        scratch_shapes=[pltpu.VMEM((SUBLANES, tc + HALO, d), F32), pltpu.VMEM((tc, d), F32)],
        compiler_params=pltpu.CompilerParams(
            dimension_semantics=("parallel",),
            vmem_limit_bytes=VMEM_LIMIT),
        name="dwconv_ln",
    )(u, u, w_rep, b_rep, ln_g, ln_b)


def _linear_residual_kernel(*refs, has_bias):
    if has_bias:
        x_ref, a_ref, w_ref, b_ref, o_ref = refs
    else:
        x_ref, a_ref, w_ref, o_ref = refs
    y = jnp.dot(a_ref[...], w_ref[...], preferred_element_type=F32)
    if has_bias:
        y = y + b_ref[...]
    o_ref[...] = x_ref[...] + y


def _linear_residual(x, a, w_all, layer, b=None, *, tm=1024, tn=1024):
    m, n = x.shape
    k = a.shape[1]
    in_specs = [
        pl.BlockSpec((tm, tn), lambda i, j: (i, j)),
        pl.BlockSpec((tm, k), lambda i, j: (i, 0)),
        pl.BlockSpec((None, k, tn), lambda i, j: (layer, 0, j)),
    ]
    args = [x, a, w_all]
    if b is not None:
        in_specs.append(pl.BlockSpec((1, tn), lambda i, j: (0, j)))
        args.append(b)
    return pl.pallas_call(
        functools.partial(_linear_residual_kernel, has_bias=b is not None),
        grid=(m // tm, n // tn),
        in_specs=in_specs,
        out_specs=pl.BlockSpec((tm, tn), lambda i, j: (i, j)),
        out_shape=jax.ShapeDtypeStruct((m, n), F32),
        compiler_params=pltpu.CompilerParams(
            dimension_semantics=("parallel", "arbitrary"),
            vmem_limit_bytes=VMEM_LIMIT),
        name="linear_residual",
    )(*args)


def _mlp_kernel(*refs, tm, final_norm):
    if final_norm:
        x_ref, g_ref, w1_ref, w2_ref, fg_ref, o_ref, xn_ref = refs
    else:
        x_ref, g_ref, w1_ref, w2_ref, o_ref, xn_ref = refs
    f = pl.program_id(1)

    @pl.when(f == 0)
    def _():
        _rmsnorm_into(x_ref, g_ref, xn_ref, tm)
        o_ref[...] = x_ref[...]

    h = jnp.dot(xn_ref[...], w1_ref[...], preferred_element_type=F32)
    h = jnp.square(jnp.maximum(h, 0.0)).astype(BF16)
    o_ref[...] += jnp.dot(h, w2_ref[...], preferred_element_type=F32)

    if final_norm:
        @pl.when(f == pl.num_programs(1) - 1)
        def _():
            _rmsnorm_into(o_ref, fg_ref, o_ref, tm)


def _mlp(x, g, w1_all, w2_all, layer, final_g=None, *, tm=1024, tf=512):
    m, d = x.shape
    dff = w1_all.shape[2]
    final_norm = final_g is not None
    in_specs = [
        pl.BlockSpec((tm, d), lambda i, f: (i, 0)),
        pl.BlockSpec((1, d), lambda i, f: (0, 0)),
        pl.BlockSpec((None, d, tf), lambda i, f: (layer, 0, f)),
        pl.BlockSpec((None, tf, d), lambda i, f: (layer, f, 0)),
    ]
    args = [x, g, w1_all, w2_all]
    if final_norm:
        in_specs.append(pl.BlockSpec((1, d), lambda i, f: (0, 0)))
        args.append(final_g)
    return pl.pallas_call(
        functools.partial(_mlp_kernel, tm=tm, final_norm=final_norm),
        grid=(m // tm, dff // tf),
        in_specs=in_specs,
        out_specs=pl.BlockSpec((tm, d), lambda i, f: (i, 0)),
        out_shape=jax.ShapeDtypeStruct((m, d), F32),
        scratch_shapes=[pltpu.VMEM((tm, d), BF16)],
        compiler_params=pltpu.CompilerParams(
            dimension_semantics=("parallel", "arbitrary"),
            vmem_limit_bytes=VMEM_LIMIT),
        name="mlp",
    )(*args)


def _rope_table_kernel(pos_ref, invf_ref, cos_t_ref, sin_t_ref, cos_r_ref, sin_r_ref):
    ang = pos_ref[...].astype(F32) * invf_ref[...]
    c = jnp.cos(ang)
    s = jnp.sin(ang)
    cos_t_ref[...] = c
    sin_t_ref[...] = s
    z = jnp.zeros((LANES - QK_ROPE_DIM, ang.shape[1]), F32)
    cos_r_ref[...] = jnp.concatenate([c, c, z], axis=0).T
    sin_r_ref[...] = jnp.concatenate([s, s, z], axis=0).T


def _rope_tables(positions, *, tr=512):
    half = QK_ROPE_DIM // 2
    m = positions.size
    inv_freq = ROPE_THETA ** (-jnp.arange(0, QK_ROPE_DIM, 2, dtype=F32) / QK_ROPE_DIM)
    return pl.pallas_call(
        _rope_table_kernel,
        grid=(m // tr,),
        in_specs=[pl.BlockSpec((1, tr), lambda i: (0, i)),
                  pl.BlockSpec((half, 1), lambda i: (0, 0))],
        out_specs=[pl.BlockSpec((half, tr), lambda i: (0, i)),
                   pl.BlockSpec((half, tr), lambda i: (0, i)),
                   pl.BlockSpec((tr, LANES), lambda i: (i, 0)),
                   pl.BlockSpec((tr, LANES), lambda i: (i, 0))],
        out_shape=[jax.ShapeDtypeStruct((half, m), F32)] * 2
                  + [jax.ShapeDtypeStruct((m, LANES), F32)] * 2,
        compiler_params=pltpu.CompilerParams(dimension_semantics=("parallel",)),
        name="rope_tables",
    )(positions.reshape(1, m), inv_freq.reshape(half, 1))


def _mla_proj_kernel(x_ref, g_ref, cos_t_ref, sin_t_ref, cos_r_ref, sin_r_ref, win_ref, qg_ref, kvg_ref,
                     wq_ref, wqs_ref, wkn_ref, wv_ref, q_ref, k_ref, v_ref, *, q_scale):
    xn = _rms_rows(x_ref[...], g_ref[...], NORM_EPS).astype(BF16)
    down = jnp.dot(xn, win_ref[...], preferred_element_type=F32)
    c_q = down[:, :Q_LORA_RANK]
    c_kv = down[:, Q_LORA_RANK:Q_LORA_RANK + KV_LORA_RANK]
    base = Q_LORA_RANK + KV_LORA_RANK
    kpe = (down[:, base:base + LANES] * cos_r_ref[...]
           + down[:, base + LANES:base + 2 * LANES] * sin_r_ref[...]).astype(k_ref.dtype)

    cqn = _rms_rows(c_q, qg_ref[...], NORM_EPS).astype(BF16)
    ckvn = _rms_rows(c_kv, kvg_ref[...], NORM_EPS).astype(BF16)

    q = lax.dot_general(wq_ref[...], cqn, _NT, preferred_element_type=F32)
    qs = lax.dot_general(wqs_ref[...], cqn, _NT, preferred_element_type=F32)
    cos2 = jnp.concatenate([cos_t_ref[...]] * 2, axis=0)
    sin2 = jnp.concatenate([sin_t_ref[...]] * 2, axis=0)
    zpad = jnp.zeros((QK_PAD_DIM - QK_HEAD_DIM, q.shape[1]), q_ref.dtype)
    for h in range(N_HEADS):
        src, dst = h * QK_HEAD_DIM, h * QK_PAD_DIM
        q_ref[dst:dst + QK_NOPE_DIM, :] = (q[src:src + QK_NOPE_DIM] * q_scale).astype(q_ref.dtype)
        rot = (q[src + QK_NOPE_DIM:src + QK_HEAD_DIM] * cos2
               + qs[h * QK_ROPE_DIM:(h + 1) * QK_ROPE_DIM] * sin2)
        q_ref[dst + QK_NOPE_DIM:dst + QK_HEAD_DIM, :] = (rot * q_scale).astype(q_ref.dtype)
        q_ref[dst + QK_HEAD_DIM:dst + QK_PAD_DIM, :] = zpad

    kn = jnp.dot(ckvn, wkn_ref[...], preferred_element_type=F32).astype(k_ref.dtype)
    for h in range(N_HEADS):
        dst = h * QK_PAD_DIM
        k_ref[:, dst:dst + QK_NOPE_DIM] = kn[:, h * QK_NOPE_DIM:(h + 1) * QK_NOPE_DIM]
        k_ref[:, dst + QK_NOPE_DIM:dst + QK_PAD_DIM] = kpe
    v_ref[...] = lax.dot_general(wv_ref[...], ckvn, _NT, preferred_element_type=F32).astype(v_ref.dtype)


def _mla_proj(x, g, tables, weights, layer, q_g, kv_g, *, q_scale, tm=256):
    m, d = x.shape
    cos_t, sin_t, cos_r, sin_r = tables

    def resident(a):
        return pl.BlockSpec(a.shape, lambda i: (0,) * a.ndim)

    def layer_resident(a):
        return pl.BlockSpec((None,) + a.shape[1:], lambda i: (layer,) + (0,) * (a.ndim - 1))

    def rows(width):
        return pl.BlockSpec((tm, width), lambda i: (i, 0))

    def cols(height):
        return pl.BlockSpec((height, tm), lambda i: (0, i))

    half = QK_ROPE_DIM // 2
    return pl.pallas_call(
        functools.partial(_mla_proj_kernel, q_scale=q_scale),
        grid=(m // tm,),
        in_specs=[rows(d), resident(g), cols(half), cols(half), rows(LANES), rows(LANES),
                  layer_resident(weights[0]), resident(q_g), resident(kv_g)]
                 + [layer_resident(w) for w in weights[1:]],
        out_specs=[cols(N_HEADS * QK_PAD_DIM), rows(N_HEADS * QK_PAD_DIM),
                   pl.BlockSpec((None, N_HEADS * V_HEAD_DIM, tm), lambda i: (i, 0, 0))],
        out_shape=[jax.ShapeDtypeStruct((N_HEADS * QK_PAD_DIM, m), BF16),
                   jax.ShapeDtypeStruct((m, N_HEADS * QK_PAD_DIM), BF16),
                   jax.ShapeDtypeStruct((m // tm, N_HEADS * V_HEAD_DIM, tm), BF16)],
        compiler_params=pltpu.CompilerParams(
            dimension_semantics=("parallel",),
            vmem_limit_bytes=VMEM_LIMIT),
        name="mla_proj",
    )(x, g, cos_t, sin_t, cos_r, sin_r, weights[0], q_g, kv_g, *weights[1:])


def _attn_kernel(q_ref, k_ref, v_ref, o_ref, s_scr, m_scr, l_scr, acc_scr, *, tq, tk, tv, hb):
    qi = pl.program_id(2)
    nv = tk // tv
    m_scr[...] = jnp.full(m_scr.shape, -jnp.inf, F32)
    l_scr[...] = jnp.zeros(l_scr.shape, F32)
    acc_scr[...] = jnp.zeros(acc_scr.shape, F32)

    def scores(kb, c):
        r = pl.multiple_of(kb * tk, tk)
        return jnp.dot(k_ref[pl.ds(r, tk), c * QK_PAD_DIM:(c + 1) * QK_PAD_DIM],
                       q_ref[c * QK_PAD_DIM:(c + 1) * QK_PAD_DIM, :], preferred_element_type=F32)

    def update(kb, c, s):
        m_prev = m_scr[c]
        m_new = jnp.maximum(m_prev, jnp.max(s, axis=0, keepdims=True))
        alpha = jnp.exp2(m_prev - m_new)
        p = jnp.exp2(s - m_new)
        l_scr[c] = alpha * l_scr[c] + jnp.sum(p, axis=0, keepdims=True)
        m_scr[c] = m_new
        p = p.astype(BF16)
        vs = slice(c * V_HEAD_DIM, (c + 1) * V_HEAD_DIM)
        pv = jnp.dot(v_ref[kb * nv, vs, :], p[:tv], preferred_element_type=F32)
        for t in range(1, nv):
            pv = pv + jnp.dot(v_ref[kb * nv + t, vs, :], p[t * tv:(t + 1) * tv],
                              preferred_element_type=F32)
        acc_scr[c] = alpha * acc_scr[c] + pv

    for c in range(hb):
        s_scr[c] = scores(0, c)

    def body(kb, carry):
        for c in range(hb):
            update(kb, c, s_scr[c])
            s_scr[c] = scores(kb + 1, c)
        return carry

    lax.fori_loop(0, qi, body, 0)
    k_chunk = lax.broadcasted_iota(jnp.int32, (tk, tq), 0) // CHUNK
    q_chunk = lax.broadcasted_iota(jnp.int32, (tk, tq), 1) // CHUNK
    visible = k_chunk <= q_chunk
    for c in range(hb):
        update(qi, c, jnp.where(visible, s_scr[c], -jnp.inf))
        o_ref[:, c * V_HEAD_DIM:(c + 1) * V_HEAD_DIM] = (acc_scr[c] / l_scr[c]).T.astype(o_ref.dtype)


def _attention(q_t, k, v_t, *, batch, seq, tq=512, hb=4):
    tk = tq
    nq = seq // tq
    tv = v_t.shape[2]
    return pl.pallas_call(
        functools.partial(_attn_kernel, tq=tq, tk=tk, tv=tv, hb=hb),
        grid=(batch, N_HEADS // hb, nq),
        in_specs=[
            pl.BlockSpec((hb * QK_PAD_DIM, tq), lambda b, h, i: (h, b * nq + i)),
            pl.BlockSpec((seq, hb * QK_PAD_DIM), lambda b, h, i: (b, h)),
            pl.BlockSpec((seq // tv, hb * V_HEAD_DIM, tv), lambda b, h, i: (b, h, 0)),
        ],
        out_specs=pl.BlockSpec((tq, hb * V_HEAD_DIM), lambda b, h, i: (b * nq + i, h)),
        out_shape=jax.ShapeDtypeStruct((batch * seq, N_HEADS * V_HEAD_DIM), BF16),
        scratch_shapes=[pltpu.VMEM((hb, tk, tq), F32), pltpu.VMEM((hb, 1, tq), F32),
                        pltpu.VMEM((hb, 1, tq), F32), pltpu.VMEM((hb, V_HEAD_DIM, tq), F32)],
        compiler_params=pltpu.CompilerParams(
            dimension_semantics=("parallel", "parallel", "arbitrary"),
            vmem_limit_bytes=VMEM_LIMIT),
        name="mla_attention",
    )(q_t, k, v_t)


def _prep_mla_weights(w_in, w_q_up, w_kv_up):
    nl, d, _ = w_in.shape
    half = QK_ROPE_DIM // 2
    base = Q_LORA_RANK + KV_LORA_RANK
    pad = jnp.zeros((nl, d, LANES - QK_ROPE_DIM), w_in.dtype)
    k_pe = w_in[:, :, base:]
    w_in_p = jnp.concatenate(
        [w_in[:, :, :base], k_pe, pad, -k_pe[:, :, half:], k_pe[:, :, :half], pad], axis=2)

    r = w_q_up.shape[1]
    pe = w_q_up.reshape(nl, r, N_HEADS, QK_HEAD_DIM)[:, :, :, QK_NOPE_DIM:]
    wq_t = jnp.swapaxes(w_q_up, 1, 2)
    wqs_t = jnp.swapaxes(
        jnp.concatenate([-pe[..., half:], pe[..., :half]], axis=3).reshape(nl, r, -1), 1, 2)

    rk = w_kv_up.shape[1]
    wkv4 = w_kv_up.reshape(nl, rk, N_HEADS, QK_NOPE_DIM + V_HEAD_DIM)
    wkn_p = wkv4[..., :QK_NOPE_DIM].reshape(nl, rk, N_HEADS * QK_NOPE_DIM)
    wv_t = jnp.swapaxes(wkv4[..., QK_NOPE_DIM:].reshape(nl, rk, N_HEADS * V_HEAD_DIM), 1, 2)
    return tuple(w.astype(BF16) for w in (w_in_p, wq_t, wqs_t, wkn_p, wv_t))


def kernel(x, positions, norm_mixer_g, norm_mlp_g, conv_w_pw1, conv_b_pw1, conv_w_dw, conv_b_dw, conv_ln_g, conv_ln_b, conv_w_pw2, conv_b_pw2, mla_w_in, mla_q_norm_g, mla_kv_norm_g, mla_w_q_up, mla_w_kv_up, mla_w_o, mlp_w1, mlp_w2, final_norm_g):
    batch, seq, d = x.shape
    m = batch * seq
    xs = x.reshape(m, d)

    def row(v):
        return v.reshape(1, -1)

    w_pw1, w_pw2, w_o, w1, w2 = (w.astype(BF16) for w in (conv_w_pw1, conv_w_pw2, mla_w_o, mlp_w1, mlp_w2))
    mla_weights = _prep_mla_weights(mla_w_in, mla_w_q_up, mla_w_kv_up)
    tables = _rope_tables(positions)
    q_scale = QK_HEAD_DIM ** -0.5 * math.log2(math.e)

    for layer in range(DEPTH):
        j = layer // N_MIXERS
        g_mix = row(norm_mixer_g[layer])
        if layer % N_MIXERS == 0:
            u = _pw1_glu(xs, g_mix, w_pw1, j, row(conv_b_pw1[j]))
            a = _dwconv_ln(u, conv_w_dw[j], row(conv_b_dw[j]), row(conv_ln_g[j]),
                           row(conv_ln_b[j]), seq=seq)
            xs = _linear_residual(xs, a, w_pw2, j, row(conv_b_pw2[j]))
        else:
            q, k, v = _mla_proj(xs, g_mix, tables, mla_weights, j, row(mla_q_norm_g[j]),
                                row(mla_kv_norm_g[j]), q_scale=q_scale)
            o = _attention(q, k, v, batch=batch, seq=seq)
            xs = _linear_residual(xs, o, w_o, j)
        final_g = row(final_norm_g) if layer == DEPTH - 1 else None
        xs = _mlp(xs, row(norm_mlp_g[layer]), w1, w2, layer, final_g)
    return xs.reshape(batch, seq, d)
```

```python
import functools
import math

import jax
import jax.numpy as jnp
from jax import lax
from jax.experimental import pallas as pl
from jax.experimental.pallas import tpu as pltpu

D_MODEL = 2048
DEPTH = 4
CHUNK = 64
N_MIXERS = 2
CONV_WIDTH = 31
N_HEADS = 16
QK_NOPE_DIM = 128
QK_ROPE_DIM = 64
V_HEAD_DIM = 128
Q_LORA_RANK = 512
KV_LORA_RANK = 512
D_FF = 4 * D_MODEL
ROPE_THETA = 10000.0
NORM_EPS = 1e-6
LN_EPS = 1e-5

SUBLANES = 8
LANES = 128
QK_HEAD_DIM = QK_NOPE_DIM + QK_ROPE_DIM
V_PAD_DIM = V_HEAD_DIM + 16
QK_PAD_DIM = 2 * LANES
HALO = 32
CONV_ROWS = 32
CONV_COLS = 512
VMEM_LIMIT = 56 * 1024 * 1024

F32 = jnp.float32
BF16 = jnp.bfloat16

_NT = (((1,), (1,)), ((), ()))


def _rms_rows(x, g, eps):
    ms = jnp.mean(x * x, axis=-1, keepdims=True)
    return x * lax.rsqrt(ms + eps) * g


def _rmsnorm_into(x_ref, g_ref, out_ref, rows, chunk=256):
    g = g_ref[...]

    def body(c, carry):
        r = pl.multiple_of(c * chunk, chunk)
        x = x_ref[pl.ds(r, chunk), :]
        out_ref[pl.ds(r, chunk), :] = _rms_rows(x, g, NORM_EPS).astype(out_ref.dtype)
        return carry

    lax.fori_loop(0, rows // chunk, body, 0)


def _cast_specs(casts, grid):
    gi, gj = grid
    in_specs, out_specs, out_shapes, args = [], [], [], []
    for w_all, layer in casts:
        _, r, c = w_all.shape
        blk = (r // gi, c // gj)
        in_specs.append(pl.BlockSpec((None,) + blk, lambda i, j, layer=layer: (layer, i, j)))
        out_specs.append(pl.BlockSpec(blk, lambda i, j: (i, j)))
        out_shapes.append(jax.ShapeDtypeStruct((r, c), BF16))
        args.append(w_all)
    return in_specs, out_specs, out_shapes, args


def _emit_casts(cast_in, cast_out):
    for src, dst in zip(cast_in, cast_out):
        dst[...] = src[...].astype(dst.dtype)


def _pw1_glu_kernel(*refs, tm, n_casts):
    x_ref, g_ref, wa_ref, wg_ref, ba_ref, bg_ref = refs[:6]
    cast_in = refs[6:6 + n_casts]
    o_ref = refs[6 + n_casts]
    cast_out = refs[7 + n_casts:7 + 2 * n_casts]
    xn_ref = refs[-1]

    @pl.when(pl.program_id(1) == 0)
    def _():
        _rmsnorm_into(x_ref, g_ref, xn_ref, tm)

    xn = xn_ref[...]
    a = jnp.dot(xn, wa_ref[...], preferred_element_type=F32) + ba_ref[...]
    gate = jnp.dot(xn, wg_ref[...], preferred_element_type=F32) + bg_ref[...]
    o_ref[...] = a * jax.nn.sigmoid(gate)
    _emit_casts(cast_in, cast_out)


def _pw1_glu(x, g, w_all, layer, b, casts=(), *, tm=1024, tn=512):
    m, d = x.shape
    nj = d // tn
    grid = (m // tm, nj)
    c_in, c_out, c_shapes, c_args = _cast_specs(casts, grid)
    outs = pl.pallas_call(
        functools.partial(_pw1_glu_kernel, tm=tm, n_casts=len(casts)),
        grid=grid,
        in_specs=[
            pl.BlockSpec((tm, d), lambda i, j: (i, 0)),
            pl.BlockSpec((1, d), lambda i, j: (0, 0)),
            pl.BlockSpec((None, d, tn), lambda i, j: (layer, 0, j)),
            pl.BlockSpec((None, d, tn), lambda i, j: (layer, 0, j + nj)),
            pl.BlockSpec((1, tn), lambda i, j: (0, j)),
            pl.BlockSpec((1, tn), lambda i, j: (0, j + nj)),
        ] + c_in,
        out_specs=[pl.BlockSpec((tm, tn), lambda i, j: (i, j))] + c_out,
        out_shape=[jax.ShapeDtypeStruct((m, d), F32)] + c_shapes,
        scratch_shapes=[pltpu.VMEM((tm, d), BF16)],
        compiler_params=pltpu.CompilerParams(
            dimension_semantics=("parallel", "arbitrary"),
            vmem_limit_bytes=VMEM_LIMIT),
        name="pw1_glu",
    )(x, g, w_all, w_all, b, b, *c_args)
    return outs[0], tuple(outs[1:])


def _dwconv_ln_kernel(u_ref, halo_ref, w_ref, b_ref, lg_ref, lb_ref, o_ref, sh_ref, conv_ref,
                      *, tc, tiles_per_seq):
    d = u_ref.shape[1]
    first = (pl.program_id(0) % tiles_per_seq) == 0
    halo = halo_ref[...]
    sh_ref[0, 0:HALO, :] = jnp.where(first, jnp.zeros_like(halo), halo)
    sh_ref[0, HALO:, :] = u_ref[...]
    n = tc + HALO - SUBLANES
    for s in range(1, SUBLANES):
        sh_ref[s, 0:n, :] = sh_ref[0, s:s + n, :]

    off = HALO - (CONV_WIDTH - 1)

    def row_body(ri, carry):
        r = pl.multiple_of(ri * CONV_ROWS, CONV_ROWS)
        groups = range(0, CONV_ROWS, SUBLANES)
        for c in range(0, d, CONV_COLS):
            cs = slice(c, c + CONV_COLS)
            bias = b_ref[:, cs]
            accs = [bias for _ in groups]
            for k in range(CONV_WIDTH):
                a, s = divmod(k + off, SUBLANES)
                wk = w_ref[k, :, cs]
                accs = [acc + sh_ref[s, pl.ds(r + SUBLANES * a + g, SUBLANES), cs] * wk
                        for acc, g in zip(accs, groups)]
            for acc, g in zip(accs, groups):
                conv_ref[pl.ds(r + g, SUBLANES), cs] = acc
        return carry

    lax.fori_loop(0, tc // CONV_ROWS, row_body, 0)

    y = conv_ref[...]
    mu = jnp.mean(y, axis=-1, keepdims=True)
    yc = y - mu
    var = jnp.mean(yc * yc, axis=-1, keepdims=True)
    z = yc * lax.rsqrt(var + LN_EPS) * lg_ref[...] + lb_ref[...]
    o_ref[...] = (z * jax.nn.sigmoid(z)).astype(o_ref.dtype)


def _dwconv_ln(u, w_dw, b_dw, ln_g, ln_b, *, seq, tc=256):
    m, d = u.shape
    halo_blocks = tc // HALO
    w_rep = jnp.broadcast_to(w_dw[:, None, :], (CONV_WIDTH, SUBLANES, d))
    b_rep = jnp.broadcast_to(b_dw, (SUBLANES, d))
    return pl.pallas_call(
        functools.partial(_dwconv_ln_kernel, tc=tc, tiles_per_seq=seq // tc),
        grid=(m // tc,),
        in_specs=[
            pl.BlockSpec((tc, d), lambda i: (i, 0)),
            pl.BlockSpec((HALO, d), lambda i: (jnp.maximum(i * halo_blocks - 1, 0), 0)),
            pl.BlockSpec((CONV_WIDTH, SUBLANES, d), lambda i: (0, 0, 0)),
            pl.BlockSpec((SUBLANES, d), lambda i: (0, 0)),
            pl.BlockSpec((1, d), lambda i: (0, 0)),
            pl.BlockSpec((1, d), lambda i: (0, 0)),
        ],
        out_specs=pl.BlockSpec((tc, d), lambda i: (i, 0)),
        out_shape=jax.ShapeDtypeStruct((m, d), BF16),
        scratch_shapes=[pltpu.VMEM((SUBLANES, tc + HALO, d), F32), pltpu.VMEM((tc, d), F32)],
        compiler_params=pltpu.CompilerParams(
            dimension_semantics=("parallel",),
            vmem_limit_bytes=VMEM_LIMIT),
        name="dwconv_ln",
    )(u, u, w_rep, b_rep, ln_g, ln_b)


def _linear_residual_kernel(*refs, has_bias):
    if has_bias:
        x_ref, a_ref, w_ref, b_ref, o_ref = refs
    else:
        x_ref, a_ref, w_ref, o_ref = refs
    y = jnp.dot(a_ref[...], w_ref[...], preferred_element_type=F32)
    if has_bias:
        y = y + b_ref[...]
    o_ref[...] = x_ref[...] + y


def _linear_residual(x, a, w_all, layer, b=None, *, tm=512, tn=D_MODEL):
    m, n = x.shape
    k = a.shape[1]
    in_specs = [
        pl.BlockSpec((tm, tn), lambda i, j: (i, j)),
        pl.BlockSpec((tm, k), lambda i, j: (i, 0)),
        pl.BlockSpec((None, k, tn), lambda i, j: (layer, 0, j)),
    ]
    args = [x, a, w_all]
    if b is not None:
        in_specs.append(pl.BlockSpec((1, tn), lambda i, j: (0, j)))
        args.append(b)
    return pl.pallas_call(
        functools.partial(_linear_residual_kernel, has_bias=b is not None),
        grid=(m // tm, n // tn),
        in_specs=in_specs,
        out_specs=pl.BlockSpec((tm, tn), lambda i, j: (i, j)),
        out_shape=jax.ShapeDtypeStruct((m, n), F32),
        compiler_params=pltpu.CompilerParams(
            dimension_semantics=("parallel", "arbitrary"),
            vmem_limit_bytes=VMEM_LIMIT),
        name="linear_residual",
    )(*args)


def _mlp_kernel(*refs, tm, final_norm, n_casts):
    n_in = 4 + int(final_norm)
    x_ref, g_ref, w1_ref, w2_ref = refs[:4]
    cast_in = refs[n_in:n_in + n_casts]
    o_ref = refs[n_in + n_casts]
    cast_out = refs[n_in + n_casts + 1:n_in + 2 * n_casts + 1]
    xn_ref = refs[-1]
    f = pl.program_id(1)

    @pl.when(f == 0)
    def _():
        _rmsnorm_into(x_ref, g_ref, xn_ref, tm)
        o_ref[...] = x_ref[...]

    h = jnp.dot(xn_ref[...], w1_ref[...], preferred_element_type=F32)
    h = jnp.square(jnp.maximum(h, 0.0)).astype(BF16)
    o_ref[...] += jnp.dot(h, w2_ref[...], preferred_element_type=F32)
    _emit_casts(cast_in, cast_out)

    if final_norm:
        fg_ref = refs[4]

        @pl.when(f == pl.num_programs(1) - 1)
        def _():
            _rmsnorm_into(o_ref, fg_ref, o_ref, tm)


def _mlp(x, g, w1, w2, final_g=None, casts=(), *, tm=1024, tf=512):
    m, d = x.shape
    dff = w1.shape[1]
    final_norm = final_g is not None
    grid = (m // tm, dff // tf)
    c_in, c_out, c_shapes, c_args = _cast_specs(casts, grid)
    in_specs = [
        pl.BlockSpec((tm, d), lambda i, f: (i, 0)),
        pl.BlockSpec((1, d), lambda i, f: (0, 0)),
        pl.BlockSpec((d, tf), lambda i, f: (0, f)),
        pl.BlockSpec((tf, d), lambda i, f: (f, 0)),
    ]
    args = [x, g, w1, w2]
    if final_norm:
        in_specs.append(pl.BlockSpec((1, d), lambda i, f: (0, 0)))
        args.append(final_g)
    outs = pl.pallas_call(
        functools.partial(_mlp_kernel, tm=tm, final_norm=final_norm, n_casts=len(casts)),
        grid=grid,
        in_specs=in_specs + c_in,
        out_specs=[pl.BlockSpec((tm, d), lambda i, f: (i, 0))] + c_out,
        out_shape=[jax.ShapeDtypeStruct((m, d), F32)] + c_shapes,
        scratch_shapes=[pltpu.VMEM((tm, d), BF16)],
        compiler_params=pltpu.CompilerParams(
            dimension_semantics=("parallel", "arbitrary"),
            vmem_limit_bytes=VMEM_LIMIT),
        name="mlp",
    )(*args, *c_args)
    return outs[0], tuple(outs[1:])


def _rope_table_kernel(pos_ref, invf_ref, cos_t_ref, sin_t_ref, cos_r_ref, sin_r_ref):
    ang = pos_ref[...].astype(F32) * invf_ref[...]
    c = jnp.cos(ang)
    s = jnp.sin(ang)
    cos_t_ref[...] = c
    sin_t_ref[...] = s
    z = jnp.zeros((LANES - QK_ROPE_DIM, ang.shape[1]), F32)
    cos_r_ref[...] = jnp.concatenate([c, c, z], axis=0).T
    sin_r_ref[...] = jnp.concatenate([s, s, z], axis=0).T


def _rope_tables(positions, *, tr=512):
    half = QK_ROPE_DIM // 2
    m = positions.size
    inv_freq = ROPE_THETA ** (-jnp.arange(0, QK_ROPE_DIM, 2, dtype=F32) / QK_ROPE_DIM)
    return pl.pallas_call(
        _rope_table_kernel,
        grid=(m // tr,),
        in_specs=[pl.BlockSpec((1, tr), lambda i: (0, i)),
                  pl.BlockSpec((half, 1), lambda i: (0, 0))],
        out_specs=[pl.BlockSpec((half, tr), lambda i: (0, i)),
                   pl.BlockSpec((half, tr), lambda i: (0, i)),
                   pl.BlockSpec((tr, LANES), lambda i: (i, 0)),
                   pl.BlockSpec((tr, LANES), lambda i: (i, 0))],
        out_shape=[jax.ShapeDtypeStruct((half, m), F32)] * 2
                  + [jax.ShapeDtypeStruct((m, LANES), F32)] * 2,
        compiler_params=pltpu.CompilerParams(dimension_semantics=("parallel",)),
        name="rope_tables",
    )(positions.reshape(1, m), inv_freq.reshape(half, 1))


def _mla_proj_kernel(x_ref, g_ref, cos_t_ref, sin_t_ref, cos_r_ref, sin_r_ref, win_ref, qg_ref, kvg_ref,
                     wq_ref, wqs_ref, wkn_ref, wv_ref, q_ref, k_ref, v_ref, *, q_scale):
    xn = _rms_rows(x_ref[...], g_ref[...], NORM_EPS).astype(BF16)
    down = jnp.dot(xn, win_ref[...], preferred_element_type=F32)
    c_q = down[:, :Q_LORA_RANK]
    c_kv = down[:, Q_LORA_RANK:Q_LORA_RANK + KV_LORA_RANK]
    base = Q_LORA_RANK + KV_LORA_RANK
    kpe = (down[:, base:base + LANES] * cos_r_ref[...]
           + down[:, base + LANES:base + 2 * LANES] * sin_r_ref[...]).astype(k_ref.dtype)

    cqn = _rms_rows(c_q, qg_ref[...], NORM_EPS).astype(BF16)
    ckvn = _rms_rows(c_kv, kvg_ref[...], NORM_EPS).astype(BF16)

    q = lax.dot_general(wq_ref[...], cqn, _NT, preferred_element_type=F32)
    qs = lax.dot_general(wqs_ref[...], cqn, _NT, preferred_element_type=F32)
    cos2 = jnp.concatenate([cos_t_ref[...]] * 2, axis=0)
    sin2 = jnp.concatenate([sin_t_ref[...]] * 2, axis=0)
    zpad = jnp.zeros((QK_PAD_DIM - QK_HEAD_DIM, q.shape[1]), q_ref.dtype)
    for h in range(N_HEADS):
        src, dst = h * QK_HEAD_DIM, h * QK_PAD_DIM
        q_ref[dst:dst + QK_NOPE_DIM, :] = (q[src:src + QK_NOPE_DIM] * q_scale).astype(q_ref.dtype)
        rot = (q[src + QK_NOPE_DIM:src + QK_HEAD_DIM] * cos2
               + qs[h * QK_ROPE_DIM:(h + 1) * QK_ROPE_DIM] * sin2)
        q_ref[dst + QK_NOPE_DIM:dst + QK_HEAD_DIM, :] = (rot * q_scale).astype(q_ref.dtype)
        q_ref[dst + QK_HEAD_DIM:dst + QK_PAD_DIM, :] = zpad

    kn = jnp.dot(ckvn, wkn_ref[...], preferred_element_type=F32).astype(k_ref.dtype)
    for h in range(N_HEADS):
        dst = h * QK_PAD_DIM
        k_ref[:, dst:dst + QK_NOPE_DIM] = kn[:, h * QK_NOPE_DIM:(h + 1) * QK_NOPE_DIM]
        k_ref[:, dst + QK_NOPE_DIM:dst + QK_PAD_DIM] = kpe
    v = lax.dot_general(wv_ref[...], ckvn, _NT, preferred_element_type=F32).astype(v_ref.dtype)
    ones = jnp.ones((V_PAD_DIM - V_HEAD_DIM, v.shape[1]), v_ref.dtype)
    for h in range(N_HEADS):
        v_ref[h * V_PAD_DIM:h * V_PAD_DIM + V_HEAD_DIM, :] = v[h * V_HEAD_DIM:(h + 1) * V_HEAD_DIM]
        v_ref[h * V_PAD_DIM + V_HEAD_DIM:(h + 1) * V_PAD_DIM, :] = ones


def _mla_proj(x, g, tables, weights, layer, q_g, kv_g, *, q_scale, tm=512):
    m, d = x.shape
    cos_t, sin_t, cos_r, sin_r = tables

    def resident(a):
        return pl.BlockSpec(a.shape, lambda i: (0,) * a.ndim)

    def layer_resident(a):
        return pl.BlockSpec((None,) + a.shape[1:], lambda i: (layer,) + (0,) * (a.ndim - 1))

    def rows(width):
        return pl.BlockSpec((tm, width), lambda i: (i, 0))

    def cols(height):
        return pl.BlockSpec((height, tm), lambda i: (0, i))

    half = QK_ROPE_DIM // 2
    return pl.pallas_call(
        functools.partial(_mla_proj_kernel, q_scale=q_scale),
        grid=(m // tm,),
        in_specs=[rows(d), resident(g), cols(half), cols(half), rows(LANES), rows(LANES),
                  layer_resident(weights[0]), resident(q_g), resident(kv_g)]
                 + [layer_resident(w) for w in weights[1:]],
        out_specs=[cols(N_HEADS * QK_PAD_DIM), rows(N_HEADS * QK_PAD_DIM),
                   pl.BlockSpec((None, N_HEADS * V_PAD_DIM, tm), lambda i: (i, 0, 0))],
        out_shape=[jax.ShapeDtypeStruct((N_HEADS * QK_PAD_DIM, m), BF16),
                   jax.ShapeDtypeStruct((m, N_HEADS * QK_PAD_DIM), BF16),
                   jax.ShapeDtypeStruct((m // tm, N_HEADS * V_PAD_DIM, tm), BF16)],
        compiler_params=pltpu.CompilerParams(
            dimension_semantics=("parallel",),
            vmem_limit_bytes=VMEM_LIMIT),
        name="mla_proj",
    )(x, g, cos_t, sin_t, cos_r, sin_r, weights[0], q_g, kv_g, *weights[1:])


def _attn_kernel(q_ref, k_ref, v_ref, o_ref, s_scr, m_scr, acc_scr, *, tq, tk, tv, hb):
    qi = pl.program_id(2)
    nv = tk // tv
    m_scr[...] = jnp.full(m_scr.shape, -jnp.inf, F32)
    acc_scr[...] = jnp.zeros(acc_scr.shape, F32)

    def scores(kb, c):
        r = pl.multiple_of(kb * tk, tk)
        return jnp.dot(k_ref[pl.ds(r, tk), c * QK_PAD_DIM:(c + 1) * QK_PAD_DIM],
                       q_ref[c * QK_PAD_DIM:(c + 1) * QK_PAD_DIM, :], preferred_element_type=F32)

    def update(kb, c, s):
        m_prev = m_scr[c]
        m_new = jnp.maximum(m_prev, jnp.max(s, axis=0, keepdims=True))
        alpha = jnp.exp2(m_prev - m_new)
        p = jnp.exp2(s - m_new).astype(BF16)
        m_scr[c] = m_new
        vs = slice(c * V_PAD_DIM, (c + 1) * V_PAD_DIM)
        pv = jnp.dot(v_ref[kb * nv, vs, :], p[:tv], preferred_element_type=F32)
        for t in range(1, nv):
            pv = pv + jnp.dot(v_ref[kb * nv + t, vs, :], p[t * tv:(t + 1) * tv],
                              preferred_element_type=F32)
        acc_scr[c] = alpha * acc_scr[c] + pv

    for c in range(hb):
        s_scr[c] = scores(0, c)

    def body(kb, carry):
        for c in range(hb):
            update(kb, c, s_scr[c])
            s_scr[c] = scores(kb + 1, c)
        return carry

    lax.fori_loop(0, qi, body, 0)
    k_chunk = lax.broadcasted_iota(jnp.int32, (tk, tq), 0) // CHUNK
    q_chunk = lax.broadcasted_iota(jnp.int32, (tk, tq), 1) // CHUNK
    visible = k_chunk <= q_chunk
    for c in range(hb):
        update(qi, c, jnp.where(visible, s_scr[c], -jnp.inf))
        acc = acc_scr[c]
        o_ref[:, c * V_HEAD_DIM:(c + 1) * V_HEAD_DIM] = (
            acc[:V_HEAD_DIM] / acc[V_HEAD_DIM:V_HEAD_DIM + 1]).T.astype(o_ref.dtype)


def _attention(q_t, k, v_t, *, batch, seq, tq=512, hb=4):
    tk = tq
    nq = seq // tq
    tv = v_t.shape[2]
    return pl.pallas_call(
        functools.partial(_attn_kernel, tq=tq, tk=tk, tv=tv, hb=hb),
        grid=(batch, N_HEADS // hb, nq),
        in_specs=[
            pl.BlockSpec((hb * QK_PAD_DIM, tq), lambda b, h, i: (h, b * nq + i)),
            pl.BlockSpec((seq, hb * QK_PAD_DIM), lambda b, h, i: (b, h)),
            pl.BlockSpec((seq // tv, hb * V_PAD_DIM, tv), lambda b, h, i: (b, h, 0)),
        ],
        out_specs=pl.BlockSpec((tq, hb * V_HEAD_DIM), lambda b, h, i: (b * nq + i, h)),
        out_shape=jax.ShapeDtypeStruct((batch * seq, N_HEADS * V_HEAD_DIM), BF16),
        scratch_shapes=[pltpu.VMEM((hb, tk, tq), F32), pltpu.VMEM((hb, 1, tq), F32),
                        pltpu.VMEM((hb, V_PAD_DIM, tq), F32)],
        compiler_params=pltpu.CompilerParams(
            dimension_semantics=("parallel", "parallel", "arbitrary"),
            vmem_limit_bytes=VMEM_LIMIT),
        name="mla_attention",
    )(q_t, k, v_t)


def _prep_mla_weights(w_in, w_q_up, w_kv_up):
    nl, d, _ = w_in.shape
    half = QK_ROPE_DIM // 2
    base = Q_LORA_RANK + KV_LORA_RANK
    pad = jnp.zeros((nl, d, LANES - QK_ROPE_DIM), w_in.dtype)
    k_pe = w_in[:, :, base:]
    w_in_p = jnp.concatenate(
        [w_in[:, :, :base], k_pe, pad, -k_pe[:, :, half:], k_pe[:, :, :half], pad], axis=2)

    r = w_q_up.shape[1]
    pe = w_q_up.reshape(nl, r, N_HEADS, QK_HEAD_DIM)[:, :, :, QK_NOPE_DIM:]
    wq_t = jnp.swapaxes(w_q_up, 1, 2)
    wqs_t = jnp.swapaxes(
        jnp.concatenate([-pe[..., half:], pe[..., :half]], axis=3).reshape(nl, r, -1), 1, 2)

    rk = w_kv_up.shape[1]
    wkv4 = w_kv_up.reshape(nl, rk, N_HEADS, QK_NOPE_DIM + V_HEAD_DIM)
    wkn_p = wkv4[..., :QK_NOPE_DIM].reshape(nl, rk, N_HEADS * QK_NOPE_DIM)
    wv_t = jnp.swapaxes(wkv4[..., QK_NOPE_DIM:].reshape(nl, rk, N_HEADS * V_HEAD_DIM), 1, 2)
    return tuple(w.astype(BF16) for w in (w_in_p, wq_t, wqs_t, wkn_p, wv_t))


def kernel(x, positions, norm_mixer_g, norm_mlp_g, conv_w_pw1, conv_b_pw1, conv_w_dw, conv_b_dw, conv_ln_g, conv_ln_b, conv_w_pw2, conv_b_pw2, mla_w_in, mla_q_norm_g, mla_kv_norm_g, mla_w_q_up, mla_w_kv_up, mla_w_o, mlp_w1, mlp_w2, final_norm_g):
    batch, seq, d = x.shape
    m = batch * seq
    xs = x.reshape(m, d)

    def row(v):
        return v.reshape(1, -1)

    w_pw1, w_pw2, w_o = (w.astype(BF16) for w in (conv_w_pw1, conv_w_pw2, mla_w_o))
    mla_weights = _prep_mla_weights(mla_w_in, mla_w_q_up, mla_w_kv_up)
    tables = _rope_tables(positions)
    q_scale = QK_HEAD_DIM ** -0.5 * math.log2(math.e)

    mlp_w = ()
    for layer in range(DEPTH):
        j = layer // N_MIXERS
        g_mix = row(norm_mixer_g[layer])
        if layer % N_MIXERS == 0:
            casts = ((mlp_w1, 0), (mlp_w2, 0)) if layer == 0 else ()
            u, cast = _pw1_glu(xs, g_mix, w_pw1, j, row(conv_b_pw1[j]), casts)
            mlp_w = cast if casts else mlp_w
            a = _dwconv_ln(u, conv_w_dw[j], row(conv_b_dw[j]), row(conv_ln_g[j]),
                           row(conv_ln_b[j]), seq=seq)
            xs = _linear_residual(xs, a, w_pw2, j, row(conv_b_pw2[j]))
        else:
            q, k, v = _mla_proj(xs, g_mix, tables, mla_weights, j, row(mla_q_norm_g[j]),
                                row(mla_kv_norm_g[j]), q_scale=q_scale)
            o = _attention(q, k, v, batch=batch, seq=seq)
            xs = _linear_residual(xs, o, w_o, j)
        last = layer == DEPTH - 1
        casts = () if last else ((mlp_w1, layer + 1), (mlp_w2, layer + 1))
        xs, mlp_w = _mlp(xs, row(norm_mlp_g[layer]), mlp_w[0], mlp_w[1],
                         row(final_norm_g) if last else None, casts)
    return xs.reshape(batch, seq, d)
```

```python
import functools
import math

import jax
import jax.numpy as jnp
from jax import lax
from jax.experimental import pallas as pl
from jax.experimental.pallas import tpu as pltpu

D_MODEL = 2048
DEPTH = 4
CHUNK = 64
N_MIXERS = 2
CONV_WIDTH = 31
N_HEADS = 16
QK_NOPE_DIM = 128
QK_ROPE_DIM = 64
V_HEAD_DIM = 128
Q_LORA_RANK = 512
KV_LORA_RANK = 512
D_FF = 4 * D_MODEL
ROPE_THETA = 10000.0
NORM_EPS = 1e-6
LN_EPS = 1e-5

SUBLANES = 8
LANES = 128
QK_HEAD_DIM = QK_NOPE_DIM + QK_ROPE_DIM
V_PAD_DIM = V_HEAD_DIM + 16
QK_PAD_DIM = 2 * LANES
HALO = 32
CONV_ROWS = 32
CONV_COLS = 512
VMEM_LIMIT = 56 * 1024 * 1024

F32 = jnp.float32
BF16 = jnp.bfloat16

_NT = (((1,), (1,)), ((), ()))


def _rms_rows(x, g, eps):
    ms = jnp.mean(x * x, axis=-1, keepdims=True)
    return x * lax.rsqrt(ms + eps) * g


def _rmsnorm_into(x_ref, g_ref, out_ref, rows, chunk=256):
    g = g_ref[...]

    def body(c, carry):
        r = pl.multiple_of(c * chunk, chunk)
        x = x_ref[pl.ds(r, chunk), :]
        out_ref[pl.ds(r, chunk), :] = _rms_rows(x, g, NORM_EPS).astype(out_ref.dtype)
        return carry

    lax.fori_loop(0, rows // chunk, body, 0)


def _cast_specs(casts, grid):
    gi, gj = grid
    in_specs, out_specs, out_shapes, args = [], [], [], []
    for w_all, layer in casts:
        _, r, c = w_all.shape
        blk = (r // gi, c // gj)
        in_specs.append(pl.BlockSpec((None,) + blk, lambda i, j, layer=layer: (layer, i, j)))
        out_specs.append(pl.BlockSpec(blk, lambda i, j: (i, j)))
        out_shapes.append(jax.ShapeDtypeStruct((r, c), BF16))
        args.append(w_all)
    return in_specs, out_specs, out_shapes, args


def _emit_casts(cast_in, cast_out):
    for src, dst in zip(cast_in, cast_out):
        dst[...] = src[...].astype(dst.dtype)


def _pw1_glu_kernel(*refs, tm, n_casts):
    x_ref, g_ref, wa_ref, wg_ref, ba_ref, bg_ref = refs[:6]
    cast_in = refs[6:6 + n_casts]
    o_ref = refs[6 + n_casts]
    cast_out = refs[7 + n_casts:7 + 2 * n_casts]
    xn_ref = refs[-1]

    @pl.when(pl.program_id(1) == 0)
    def _():
        _rmsnorm_into(x_ref, g_ref, xn_ref, tm)

    xn = xn_ref[...]
    tn = o_ref.shape[1]
    for cs in (slice(0, tn // 2), slice(tn // 2, tn)):
        a = jnp.dot(xn, wa_ref[:, cs], preferred_element_type=F32) + ba_ref[:, cs]
        gate = jnp.dot(xn, wg_ref[:, cs], preferred_element_type=F32) + bg_ref[:, cs]
        o_ref[:, cs] = a * jax.nn.sigmoid(gate)
    _emit_casts(cast_in, cast_out)


def _pw1_glu(x, g, w, b, casts=(), *, tm=1024, tn=512):
    m, d = x.shape
    nj = d // tn
    grid = (m // tm, nj)
    c_in, c_out, c_shapes, c_args = _cast_specs(casts, grid)
    outs = pl.pallas_call(
        functools.partial(_pw1_glu_kernel, tm=tm, n_casts=len(casts)),
        grid=grid,
        in_specs=[
            pl.BlockSpec((tm, d), lambda i, j: (i, 0)),
            pl.BlockSpec((1, d), lambda i, j: (0, 0)),
            pl.BlockSpec((d, tn), lambda i, j: (0, j)),
            pl.BlockSpec((d, tn), lambda i, j: (0, j + nj)),
            pl.BlockSpec((1, tn), lambda i, j: (0, j)),
            pl.BlockSpec((1, tn), lambda i, j: (0, j + nj)),
        ] + c_in,
        out_specs=[pl.BlockSpec((tm, tn), lambda i, j: (i, j))] + c_out,
        out_shape=[jax.ShapeDtypeStruct((m, d), F32)] + c_shapes,
        scratch_shapes=[pltpu.VMEM((tm, d), BF16)],
        compiler_params=pltpu.CompilerParams(
            dimension_semantics=("parallel", "arbitrary"),
            vmem_limit_bytes=VMEM_LIMIT),
        name="pw1_glu",
    )(x, g, w, w, b, b, *c_args)
    return outs[0], tuple(outs[1:])


def _dwconv_ln_kernel(u_ref, halo_ref, w_ref, b_ref, lg_ref, lb_ref, o_ref, sh_ref, conv_ref,
                      *, tc, tiles_per_seq):
    d = u_ref.shape[1]
    first = (pl.program_id(0) % tiles_per_seq) == 0
    halo = halo_ref[...]
    sh_ref[0, 0:HALO, :] = jnp.where(first, jnp.zeros_like(halo), halo)
    sh_ref[0, HALO:, :] = u_ref[...]
    n = tc + HALO - SUBLANES
    for s in range(1, SUBLANES):
        sh_ref[s, 0:n, :] = sh_ref[0, s:s + n, :]

    off = HALO - (CONV_WIDTH - 1)

    def row_body(ri, carry):
        r = pl.multiple_of(ri * CONV_ROWS, CONV_ROWS)
        groups = range(0, CONV_ROWS, SUBLANES)
        for c in range(0, d, CONV_COLS):
            cs = slice(c, c + CONV_COLS)
            bias = b_ref[:, cs]
            accs = [bias for _ in groups]
            for k in range(CONV_WIDTH):
                a, s = divmod(k + off, SUBLANES)
                wk = w_ref[k, :, cs]
                accs = [acc + sh_ref[s, pl.ds(r + SUBLANES * a + g, SUBLANES), cs] * wk
                        for acc, g in zip(accs, groups)]
            for acc, g in zip(accs, groups):
                conv_ref[pl.ds(r + g, SUBLANES), cs] = acc
        return carry

    lax.fori_loop(0, tc // CONV_ROWS, row_body, 0)

    y = conv_ref[...]
    mu = jnp.mean(y, axis=-1, keepdims=True)
    yc = y - mu
    var = jnp.mean(yc * yc, axis=-1, keepdims=True)
    z = yc * lax.rsqrt(var + LN_EPS) * lg_ref[...] + lb_ref[...]
    o_ref[...] = (z * jax.nn.sigmoid(z)).astype(o_ref.dtype)


def _dwconv_ln(u, w_dw, b_dw, ln_g, ln_b, *, seq, tc=256):
    m, d = u.shape
    halo_blocks = tc // HALO
    w_rep = jnp.broadcast_to(w_dw[:, None, :], (CONV_WIDTH, SUBLANES, d))
    b_rep = jnp.broadcast_to(b_dw, (SUBLANES, d))
    return pl.pallas_call(
        functools.partial(_dwconv_ln_kernel, tc=tc, tiles_per_seq=seq // tc),
        grid=(m // tc,),
        in_specs=[
            pl.BlockSpec((tc, d), lambda i: (i, 0)),
            pl.BlockSpec((HALO, d), lambda i: (jnp.maximum(i * halo_blocks - 1, 0), 0)),
            pl.BlockSpec((CONV_WIDTH, SUBLANES, d), lambda i: (0, 0, 0)),
            pl.BlockSpec((SUBLANES, d), lambda i: (0, 0)),
            pl.BlockSpec((1, d), lambda i: (0, 0)),
            pl.BlockSpec((1, d), lambda i: (0, 0)),
        ],
        out_specs=pl.BlockSpec((tc, d), lambda i: (i, 0)),
        out_shape=jax.ShapeDtypeStruct((m, d), BF16),
        scratch_shapes=[pltpu.VMEM((SUBLANES, tc + HALO, d), F32), pltpu.VMEM((tc, d), F32)],
        compiler_params=pltpu.CompilerParams(
            dimension_semantics=("parallel",),
            vmem_limit_bytes=VMEM_LIMIT),
        name="dwconv_ln",
    )(u, u, w_rep, b_rep, ln_g, ln_b)


def _linear_residual_kernel(*refs, has_bias):
    if has_bias:
        x_ref, a_ref, w_ref, b_ref, o_ref = refs
    else:
        x_ref, a_ref, w_ref, o_ref = refs
    y = jnp.dot(a_ref[...], w_ref[...], preferred_element_type=F32)
    if has_bias:
        y = y + b_ref[...]
    o_ref[...] = x_ref[...] + y


def _linear_residual(x, a, w, b=None, *, tm=512, tn=D_MODEL):
    m, n = x.shape
    k = a.shape[1]
    in_specs = [
        pl.BlockSpec((tm, tn), lambda i, j: (i, j)),
        pl.BlockSpec((tm, k), lambda i, j: (i, 0)),
        pl.BlockSpec((k, tn), lambda i, j: (0, j)),
    ]
    args = [x, a, w]
    if b is not None:
        in_specs.append(pl.BlockSpec((1, tn), lambda i, j: (0, j)))
        args.append(b)
    return pl.pallas_call(
        functools.partial(_linear_residual_kernel, has_bias=b is not None),
        grid=(m // tm, n // tn),
        in_specs=in_specs,
        out_specs=pl.BlockSpec((tm, tn), lambda i, j: (i, j)),
        out_shape=jax.ShapeDtypeStruct((m, n), F32),
        compiler_params=pltpu.CompilerParams(
            dimension_semantics=("parallel", "arbitrary"),
            vmem_limit_bytes=VMEM_LIMIT),
        name="linear_residual",
    )(*args)


def _mlp_kernel(*refs, tm, final_norm, n_casts):
    n_in = 4 + int(final_norm)
    x_ref, g_ref, w1_ref, w2_ref = refs[:4]
    cast_in = refs[n_in:n_in + n_casts]
    o_ref = refs[n_in + n_casts]
    cast_out = refs[n_in + n_casts + 1:n_in + 2 * n_casts + 1]
    xn_ref = refs[-1]
    f = pl.program_id(1)

    @pl.when(f == 0)
    def _():
        _rmsnorm_into(x_ref, g_ref, xn_ref, tm)
        o_ref[...] = x_ref[...]

    h = jnp.dot(xn_ref[...], w1_ref[...], preferred_element_type=F32)
    h = jnp.square(jnp.maximum(h, 0.0)).astype(BF16)
    o_ref[...] += jnp.dot(h, w2_ref[...], preferred_element_type=F32)
    _emit_casts(cast_in, cast_out)

    if final_norm:
        fg_ref = refs[4]

        @pl.when(f == pl.num_programs(1) - 1)
        def _():
            _rmsnorm_into(o_ref, fg_ref, o_ref, tm)


def _mlp(x, g, w1, w2, final_g=None, casts=(), *, tm=1024, tf=512):
    m, d = x.shape
    dff = w1.shape[1]
    final_norm = final_g is not None
    grid = (m // tm, dff // tf)
    c_in, c_out, c_shapes, c_args = _cast_specs(casts, grid)
    in_specs = [
        pl.BlockSpec((tm, d), lambda i, f: (i, 0)),
        pl.BlockSpec((1, d), lambda i, f: (0, 0)),
        pl.BlockSpec((d, tf), lambda i, f: (0, f)),
        pl.BlockSpec((tf, d), lambda i, f: (f, 0)),
    ]
    args = [x, g, w1, w2]
    if final_norm:
        in_specs.append(pl.BlockSpec((1, d), lambda i, f: (0, 0)))
        args.append(final_g)
    outs = pl.pallas_call(
        functools.partial(_mlp_kernel, tm=tm, final_norm=final_norm, n_casts=len(casts)),
        grid=grid,
        in_specs=in_specs + c_in,
        out_specs=[pl.BlockSpec((tm, d), lambda i, f: (i, 0))] + c_out,
        out_shape=[jax.ShapeDtypeStruct((m, d), F32)] + c_shapes,
        scratch_shapes=[pltpu.VMEM((tm, d), BF16)],
        compiler_params=pltpu.CompilerParams(
            dimension_semantics=("parallel", "arbitrary"),
            vmem_limit_bytes=VMEM_LIMIT),
        name="mlp",
    )(*args, *c_args)
    return outs[0], tuple(outs[1:])


def _rope_table_kernel(pos_ref, invf_ref, cos_t_ref, sin_t_ref, cos_r_ref, sin_r_ref):
    ang = pos_ref[...].astype(F32) * invf_ref[...]
    c = jnp.cos(ang)
    s = jnp.sin(ang)
    cos_t_ref[...] = c
    sin_t_ref[...] = s
    z = jnp.zeros((LANES - QK_ROPE_DIM, ang.shape[1]), F32)
    cos_r_ref[...] = jnp.concatenate([c, c, z], axis=0).T
    sin_r_ref[...] = jnp.concatenate([s, s, z], axis=0).T


def _rope_tables(positions, *, tr=512):
    half = QK_ROPE_DIM // 2
    m = positions.size
    inv_freq = ROPE_THETA ** (-jnp.arange(0, QK_ROPE_DIM, 2, dtype=F32) / QK_ROPE_DIM)
    return pl.pallas_call(
        _rope_table_kernel,
        grid=(m // tr,),
        in_specs=[pl.BlockSpec((1, tr), lambda i: (0, i)),
                  pl.BlockSpec((half, 1), lambda i: (0, 0))],
        out_specs=[pl.BlockSpec((half, tr), lambda i: (0, i)),
                   pl.BlockSpec((half, tr), lambda i: (0, i)),
                   pl.BlockSpec((tr, LANES), lambda i: (i, 0)),
                   pl.BlockSpec((tr, LANES), lambda i: (i, 0))],
        out_shape=[jax.ShapeDtypeStruct((half, m), F32)] * 2
                  + [jax.ShapeDtypeStruct((m, LANES), F32)] * 2,
        compiler_params=pltpu.CompilerParams(dimension_semantics=("parallel",)),
        name="rope_tables",
    )(positions.reshape(1, m), inv_freq.reshape(half, 1))


def _mla_proj_kernel(x_ref, g_ref, cos_t_ref, sin_t_ref, cos_r_ref, sin_r_ref, win_ref, qg_ref, kvg_ref,
                     wq_ref, wqs_ref, wkn_ref, wv_ref, q_ref, k_ref, v_ref, *, q_scale):
    xn = _rms_rows(x_ref[...], g_ref[...], NORM_EPS).astype(BF16)
    down = jnp.dot(xn, win_ref[...], preferred_element_type=F32)
    c_q = down[:, :Q_LORA_RANK]
    c_kv = down[:, Q_LORA_RANK:Q_LORA_RANK + KV_LORA_RANK]
    base = Q_LORA_RANK + KV_LORA_RANK
    kpe = (down[:, base:base + LANES] * cos_r_ref[...]
           + down[:, base + LANES:base + 2 * LANES] * sin_r_ref[...]).astype(k_ref.dtype)

    cqn = _rms_rows(c_q, qg_ref[...], NORM_EPS).astype(BF16)
    ckvn = _rms_rows(c_kv, kvg_ref[...], NORM_EPS).astype(BF16)

    q = lax.dot_general(wq_ref[...], cqn, _NT, preferred_element_type=F32)
    qs = lax.dot_general(wqs_ref[...], cqn, _NT, preferred_element_type=F32)
    cos2 = jnp.concatenate([cos_t_ref[...]] * 2, axis=0)
    sin2 = jnp.concatenate([sin_t_ref[...]] * 2, axis=0)
    zpad = jnp.zeros((QK_PAD_DIM - QK_HEAD_DIM, q.shape[1]), q_ref.dtype)
    for h in range(N_HEADS):
        src, dst = h * QK_HEAD_DIM, h * QK_PAD_DIM
        q_ref[dst:dst + QK_NOPE_DIM, :] = (q[src:src + QK_NOPE_DIM] * q_scale).astype(q_ref.dtype)
        rot = (q[src + QK_NOPE_DIM:src + QK_HEAD_DIM] * cos2
               + qs[h * QK_ROPE_DIM:(h + 1) * QK_ROPE_DIM] * sin2)
        q_ref[dst + QK_NOPE_DIM:dst + QK_HEAD_DIM, :] = (rot * q_scale).astype(q_ref.dtype)
        q_ref[dst + QK_HEAD_DIM:dst + QK_PAD_DIM, :] = zpad

    kn = jnp.dot(ckvn, wkn_ref[...], preferred_element_type=F32).astype(k_ref.dtype)
    for h in range(N_HEADS):
        dst = h * QK_PAD_DIM
        k_ref[:, dst:dst + QK_NOPE_DIM] = kn[:, h * QK_NOPE_DIM:(h + 1) * QK_NOPE_DIM]
        k_ref[:, dst + QK_NOPE_DIM:dst + QK_PAD_DIM] = kpe
    v = lax.dot_general(wv_ref[...], ckvn, _NT, preferred_element_type=F32).astype(v_ref.dtype)
    ones = jnp.ones((V_PAD_DIM - V_HEAD_DIM, v.shape[1]), v_ref.dtype)
    for h in range(N_HEADS):
        v_ref[h * V_PAD_DIM:h * V_PAD_DIM + V_HEAD_DIM, :] = v[h * V_HEAD_DIM:(h + 1) * V_HEAD_DIM]
        v_ref[h * V_PAD_DIM + V_HEAD_DIM:(h + 1) * V_PAD_DIM, :] = ones


def _mla_proj(x, g, tables, weights, layer, q_g, kv_g, *, q_scale, tm=512):
    m, d = x.shape
    cos_t, sin_t, cos_r, sin_r = tables

    def resident(a):
        return pl.BlockSpec(a.shape, lambda i: (0,) * a.ndim)

    def layer_resident(a):
        return pl.BlockSpec((None,) + a.shape[1:], lambda i: (layer,) + (0,) * (a.ndim - 1))

    def rows(width):
        return pl.BlockSpec((tm, width), lambda i: (i, 0))

    def cols(height):
        return pl.BlockSpec((height, tm), lambda i: (0, i))

    half = QK_ROPE_DIM // 2
    return pl.pallas_call(
        functools.partial(_mla_proj_kernel, q_scale=q_scale),
        grid=(m // tm,),
        in_specs=[rows(d), resident(g), cols(half), cols(half), rows(LANES), rows(LANES),
                  layer_resident(weights[0]), resident(q_g), resident(kv_g)]
                 + [layer_resident(w) for w in weights[1:]],
        out_specs=[cols(N_HEADS * QK_PAD_DIM), rows(N_HEADS * QK_PAD_DIM),
                   pl.BlockSpec((None, N_HEADS * V_PAD_DIM, tm), lambda i: (i, 0, 0))],
        out_shape=[jax.ShapeDtypeStruct((N_HEADS * QK_PAD_DIM, m), BF16),
                   jax.ShapeDtypeStruct((m, N_HEADS * QK_PAD_DIM), BF16),
                   jax.ShapeDtypeStruct((m // tm, N_HEADS * V_PAD_DIM, tm), BF16)],
        compiler_params=pltpu.CompilerParams(
            dimension_semantics=("parallel",),
            vmem_limit_bytes=VMEM_LIMIT),
        name="mla_proj",
    )(x, g, cos_t, sin_t, cos_r, sin_r, weights[0], q_g, kv_g, *weights[1:])


def _attn_kernel(q_ref, qn_ref, k_ref, v_ref, o_ref, s_scr, m_scr, acc_scr, *, tq, tk, tv, hb):
    qi = pl.program_id(2)
    nv = tk // tv
    m_scr[...] = jnp.full(m_scr.shape, -jnp.inf, F32)
    acc_scr[...] = jnp.zeros(acc_scr.shape, F32)

    def scores(kb, c):
        r = pl.multiple_of(kb * tk, tk)
        return jnp.dot(k_ref[pl.ds(r, tk), c * QK_PAD_DIM:(c + 1) * QK_PAD_DIM],
                       q_ref[c * QK_PAD_DIM:(c + 1) * QK_PAD_DIM, :], preferred_element_type=F32)

    def update(kb, c, s):
        m_prev = m_scr[c]
        m_new = jnp.maximum(m_prev, jnp.max(s, axis=0, keepdims=True))
        alpha = jnp.exp2(m_prev - m_new)
        p = jnp.exp2(s - m_new).astype(BF16)
        m_scr[c] = m_new
        vs = slice(c * V_PAD_DIM, (c + 1) * V_PAD_DIM)
        pv = jnp.dot(v_ref[kb * nv, vs, :], p[:tv], preferred_element_type=F32)
        for t in range(1, nv):
            pv = pv + jnp.dot(v_ref[kb * nv + t, vs, :], p[t * tv:(t + 1) * tv],
                              preferred_element_type=F32)
        acc_scr[c] = alpha * acc_scr[c] + pv

    def advance(kb, cur, nxt):
        for c in range(hb):
            s_scr[nxt, c] = scores(kb + 1, c)
            update(kb, c, s_scr[cur, c])

    @pl.when(qi == 0)
    def _():
        for c in range(hb):
            s_scr[0, c] = scores(0, c)

    def body(t, carry):
        advance(2 * t, 0, 1)
        advance(2 * t + 1, 1, 0)
        return carry

    lax.fori_loop(0, qi // 2, body, 0)

    @pl.when(qi % 2 == 1)
    def _():
        advance(qi - 1, 0, 1)

    k_chunk = lax.broadcasted_iota(jnp.int32, (tk, tq), 0) // CHUNK
    q_chunk = lax.broadcasted_iota(jnp.int32, (tk, tq), 1) // CHUNK
    visible = k_chunk <= q_chunk

    def finish(slot):
        for c in range(hb):
            update(qi, c, jnp.where(visible, s_scr[slot, c], -jnp.inf))
            s_scr[0, c] = jnp.dot(k_ref[0:tk, c * QK_PAD_DIM:(c + 1) * QK_PAD_DIM],
                                  qn_ref[c * QK_PAD_DIM:(c + 1) * QK_PAD_DIM, :],
                                  preferred_element_type=F32)
            acc = acc_scr[c]
            o_ref[:, c * V_HEAD_DIM:(c + 1) * V_HEAD_DIM] = (
                acc[:V_HEAD_DIM] / acc[V_HEAD_DIM:V_HEAD_DIM + 1]).T.astype(o_ref.dtype)

    for parity in range(2):
        pl.when(qi % 2 == parity)(functools.partial(finish, parity))


def _attention(q_t, k, v_t, *, batch, seq, tq=512, hb=4):
    tk = tq
    nq = seq // tq
    tv = v_t.shape[2]
    return pl.pallas_call(
        functools.partial(_attn_kernel, tq=tq, tk=tk, tv=tv, hb=hb),
        grid=(batch, N_HEADS // hb, nq),
        in_specs=[
            pl.BlockSpec((hb * QK_PAD_DIM, tq), lambda b, h, i: (h, b * nq + i)),
            pl.BlockSpec((hb * QK_PAD_DIM, tq), lambda b, h, i: (h, b * nq + jnp.minimum(i + 1, nq - 1))),
            pl.BlockSpec((seq, hb * QK_PAD_DIM), lambda b, h, i: (b, h)),
            pl.BlockSpec((seq // tv, hb * V_PAD_DIM, tv), lambda b, h, i: (b, h, 0)),
        ],
        out_specs=pl.BlockSpec((tq, hb * V_HEAD_DIM), lambda b, h, i: (b * nq + i, h)),
        out_shape=jax.ShapeDtypeStruct((batch * seq, N_HEADS * V_HEAD_DIM), BF16),
        scratch_shapes=[pltpu.VMEM((2, hb, tk, tq), F32), pltpu.VMEM((hb, 1, tq), F32),
                        pltpu.VMEM((hb, V_PAD_DIM, tq), F32)],
        compiler_params=pltpu.CompilerParams(
            dimension_semantics=("parallel", "parallel", "arbitrary"),
            vmem_limit_bytes=VMEM_LIMIT),
        name="mla_attention",
    )(q_t, q_t, k, v_t)


def _prep_mla_weights(w_in, w_q_up, w_kv_up):
    nl, d, _ = w_in.shape
    half = QK_ROPE_DIM // 2
    base = Q_LORA_RANK + KV_LORA_RANK
    pad = jnp.zeros((nl, d, LANES - QK_ROPE_DIM), w_in.dtype)
    k_pe = w_in[:, :, base:]
    w_in_p = jnp.concatenate(
        [w_in[:, :, :base], k_pe, pad, -k_pe[:, :, half:], k_pe[:, :, :half], pad], axis=2)

    r = w_q_up.shape[1]
    pe = w_q_up.reshape(nl, r, N_HEADS, QK_HEAD_DIM)[:, :, :, QK_NOPE_DIM:]
    wq_t = jnp.swapaxes(w_q_up, 1, 2)
    wqs_t = jnp.swapaxes(
        jnp.concatenate([-pe[..., half:], pe[..., :half]], axis=3).reshape(nl, r, -1), 1, 2)

    rk = w_kv_up.shape[1]
    wkv4 = w_kv_up.reshape(nl, rk, N_HEADS, QK_NOPE_DIM + V_HEAD_DIM)
    wkn_p = wkv4[..., :QK_NOPE_DIM].reshape(nl, rk, N_HEADS * QK_NOPE_DIM)
    wv_t = jnp.swapaxes(wkv4[..., QK_NOPE_DIM:].reshape(nl, rk, N_HEADS * V_HEAD_DIM), 1, 2)
    return tuple(w.astype(BF16) for w in (w_in_p, wq_t, wqs_t, wkn_p, wv_t))


def kernel(x, positions, norm_mixer_g, norm_mlp_g, conv_w_pw1, conv_b_pw1, conv_w_dw, conv_b_dw, conv_ln_g, conv_ln_b, conv_w_pw2, conv_b_pw2, mla_w_in, mla_q_norm_g, mla_kv_norm_g, mla_w_q_up, mla_w_kv_up, mla_w_o, mlp_w1, mlp_w2, final_norm_g):
    batch, seq, d = x.shape
    m = batch * seq
    xs = x.reshape(m, d)

    def row(v):
        return v.reshape(1, -1)

    mla_weights = _prep_mla_weights(mla_w_in, mla_w_q_up, mla_w_kv_up)
    tables = _rope_tables(positions)
    q_scale = QK_HEAD_DIM ** -0.5 * math.log2(math.e)

    def next_layer_casts(layer):
        casts = [(mlp_w1, layer), (mlp_w2, layer)]
        if layer % N_MIXERS == 0:
            casts.append((conv_w_pw2, layer // N_MIXERS))
            if layer > 0:
                casts.append((conv_w_pw1, layer // N_MIXERS))
        else:
            casts.append((mla_w_o, layer // N_MIXERS))
        return tuple(casts)

    w_pw1 = conv_w_pw1[0].astype(BF16)
    cast = ()
    for layer in range(DEPTH):
        j = layer // N_MIXERS
        g_mix = row(norm_mixer_g[layer])
        if layer % N_MIXERS == 0:
            if layer == 0:
                u, cast = _pw1_glu(xs, g_mix, w_pw1, row(conv_b_pw1[j]), next_layer_casts(0))
            else:
                u, _ = _pw1_glu(xs, g_mix, cast[3], row(conv_b_pw1[j]))
            a = _dwconv_ln(u, conv_w_dw[j], row(conv_b_dw[j]), row(conv_ln_g[j]),
                           row(conv_ln_b[j]), seq=seq)
            xs = _linear_residual(xs, a, cast[2], row(conv_b_pw2[j]))
        else:
            q, k, v = _mla_proj(xs, g_mix, tables, mla_weights, j, row(mla_q_norm_g[j]),
                                row(mla_kv_norm_g[j]), q_scale=q_scale)
            o = _attention(q, k, v, batch=batch, seq=seq)
            xs = _linear_residual(xs, o, cast[2])
        last = layer == DEPTH - 1
        xs, cast_next = _mlp(xs, row(norm_mlp_g[layer]), cast[0], cast[1],
                             row(final_norm_g) if last else None,
                             () if last else next_layer_casts(layer + 1))
        cast = cast_next
    return xs.reshape(batch, seq, d)
```

```python
import functools
import math

import jax
import jax.numpy as jnp
from jax import lax
from jax.experimental import pallas as pl
from jax.experimental.pallas import tpu as pltpu

D_MODEL = 2048
DEPTH = 4
CHUNK = 64
N_MIXERS = 2
CONV_WIDTH = 31
N_HEADS = 16
QK_NOPE_DIM = 128
QK_ROPE_DIM = 64
V_HEAD_DIM = 128
Q_LORA_RANK = 512
KV_LORA_RANK = 512
D_FF = 4 * D_MODEL
ROPE_THETA = 10000.0
NORM_EPS = 1e-6
LN_EPS = 1e-5

SUBLANES = 8
LANES = 128
QK_HEAD_DIM = QK_NOPE_DIM + QK_ROPE_DIM
V_PAD_DIM = V_HEAD_DIM + 16
QK_PAD_DIM = 2 * LANES
HALO = 32
CONV_ROWS = 32
CONV_COLS = 512
VMEM_LIMIT = 56 * 1024 * 1024
MLP_VMEM_LIMIT = 60 * 1024 * 1024

F32 = jnp.float32
BF16 = jnp.bfloat16

_NT = (((1,), (1,)), ((), ()))


def _rms_rows(x, g, eps):
    ms = jnp.mean(x * x, axis=-1, keepdims=True)
    return x * lax.rsqrt(ms + eps) * g


def _rmsnorm_into(x_ref, g_ref, out_ref, rows, chunk=256):
    g = g_ref[...]

    def body(c, carry):
        r = pl.multiple_of(c * chunk, chunk)
        x = x_ref[pl.ds(r, chunk), :]
        out_ref[pl.ds(r, chunk), :] = _rms_rows(x, g, NORM_EPS).astype(out_ref.dtype)
        return carry

    lax.fori_loop(0, rows // chunk, body, 0)


def _cast_specs(casts, n_steps, step_of):
    in_specs, out_specs, out_shapes, args = [], [], [], []
    for w_all, layer in casts:
        _, r, c = w_all.shape
        blk = (r // n_steps, c)
        in_specs.append(pl.BlockSpec((None,) + blk, lambda *ids, layer=layer: (layer, step_of(*ids), 0)))
        out_specs.append(pl.BlockSpec(blk, lambda *ids: (step_of(*ids), 0)))
        out_shapes.append(jax.ShapeDtypeStruct((r, c), BF16))
        args.append(w_all)
    return in_specs, out_specs, out_shapes, args


def _emit_casts(cast_in, cast_out):
    for src, dst in zip(cast_in, cast_out):
        dst[...] = src[...].astype(dst.dtype)


def _pw1_glu_kernel(x_ref, g_ref, wa_ref, wg_ref, ba_ref, bg_ref, o_ref, xn_ref, *, tm):
    @pl.when(pl.program_id(1) == 0)
    def _():
        _rmsnorm_into(x_ref, g_ref, xn_ref, tm)

    xn = xn_ref[...]
    tn = o_ref.shape[1]
    for cs in (slice(0, tn // 2), slice(tn // 2, tn)):
        a = jnp.dot(xn, wa_ref[:, cs], preferred_element_type=F32) + ba_ref[:, cs]
        gate = jnp.dot(xn, wg_ref[:, cs], preferred_element_type=F32) + bg_ref[:, cs]
        o_ref[:, cs] = a * jax.nn.sigmoid(gate)


def _pw1_glu(x, g, w, b, *, tm=1024, tn=512):
    m, d = x.shape
    nj = d // tn
    return pl.pallas_call(
        functools.partial(_pw1_glu_kernel, tm=tm),
        grid=(m // tm, nj),
        in_specs=[
            pl.BlockSpec((tm, d), lambda i, j: (i, 0)),
            pl.BlockSpec((1, d), lambda i, j: (0, 0)),
            pl.BlockSpec((d, tn), lambda i, j: (0, j)),
            pl.BlockSpec((d, tn), lambda i, j: (0, j + nj)),
            pl.BlockSpec((1, tn), lambda i, j: (0, j)),
            pl.BlockSpec((1, tn), lambda i, j: (0, j + nj)),
        ],
        out_specs=pl.BlockSpec((tm, tn), lambda i, j: (i, j)),
        out_shape=jax.ShapeDtypeStruct((m, d), F32),
        scratch_shapes=[pltpu.VMEM((tm, d), BF16)],
        compiler_params=pltpu.CompilerParams(
            dimension_semantics=("parallel", "arbitrary"),
            vmem_limit_bytes=VMEM_LIMIT),
        name="pw1_glu",
    )(x, g, w, w, b, b)


def _dwconv_ln_kernel(*refs, tc, tiles_per_seq, n_casts):
    u_ref, halo_ref, w_ref, b_ref, lg_ref, lb_ref = refs[:6]
    cast_in = refs[6:6 + n_casts]
    o_ref = refs[6 + n_casts]
    cast_out = refs[7 + n_casts:7 + 2 * n_casts]
    sh_ref, conv_ref = refs[-2:]
    _emit_casts(cast_in, cast_out)
    d = u_ref.shape[1]
    first = (pl.program_id(0) % tiles_per_seq) == 0
    halo = halo_ref[...]
    sh_ref[0, 0:HALO, :] = jnp.where(first, jnp.zeros_like(halo), halo)
    sh_ref[0, HALO:, :] = u_ref[...]
    n = tc + HALO - SUBLANES
    for s in range(1, SUBLANES):
        sh_ref[s, 0:n, :] = sh_ref[0, s:s + n, :]

    off = HALO - (CONV_WIDTH - 1)

    def row_body(ri, carry):
        r = pl.multiple_of(ri * CONV_ROWS, CONV_ROWS)
        groups = range(0, CONV_ROWS, SUBLANES)
        for c in range(0, d, CONV_COLS):
            cs = slice(c, c + CONV_COLS)
            bias = b_ref[:, cs]
            accs = [bias for _ in groups]
            for k in range(CONV_WIDTH):
                a, s = divmod(k + off, SUBLANES)
                wk = w_ref[k, :, cs]
                accs = [acc + sh_ref[s, pl.ds(r + SUBLANES * a + g, SUBLANES), cs] * wk
                        for acc, g in zip(accs, groups)]
            for acc, g in zip(accs, groups):
                conv_ref[pl.ds(r + g, SUBLANES), cs] = acc
        return carry

    lax.fori_loop(0, tc // CONV_ROWS, row_body, 0)

    y = conv_ref[...]
    mu = jnp.mean(y, axis=-1, keepdims=True)
    yc = y - mu
    var = jnp.mean(yc * yc, axis=-1, keepdims=True)
    z = yc * lax.rsqrt(var + LN_EPS) * lg_ref[...] + lb_ref[...]
    o_ref[...] = (z * jax.nn.sigmoid(z)).astype(o_ref.dtype)


def _dwconv_ln(u, w_dw, b_dw, ln_g, ln_b, casts=(), *, seq, tc=256):
    m, d = u.shape
    halo_blocks = tc // HALO
    grid = (m // tc,)
    c_in, c_out, c_shapes, c_args = _cast_specs(casts, grid[0], lambda i: i)
    w_rep = jnp.broadcast_to(w_dw[:, None, :], (CONV_WIDTH, SUBLANES, d))
    b_rep = jnp.broadcast_to(b_dw, (SUBLANES, d))
    outs = pl.pallas_call(
        functools.partial(_dwconv_ln_kernel, tc=tc, tiles_per_seq=seq // tc, n_casts=len(casts)),
        grid=grid,
        in_specs=[
            pl.BlockSpec((tc, d), lambda i: (i, 0)),
            pl.BlockSpec((HALO, d), lambda i: (jnp.maximum(i * halo_blocks - 1, 0), 0)),
            pl.BlockSpec((CONV_WIDTH, SUBLANES, d), lambda i: (0, 0, 0)),
            pl.BlockSpec((SUBLANES, d), lambda i: (0, 0)),
            pl.BlockSpec((1, d), lambda i: (0, 0)),
            pl.BlockSpec((1, d), lambda i: (0, 0)),
        ] + c_in,
        out_specs=[pl.BlockSpec((tc, d), lambda i: (i, 0))] + c_out,
        out_shape=[jax.ShapeDtypeStruct((m, d), BF16)] + c_shapes,
        scratch_shapes=[pltpu.VMEM((SUBLANES, tc + HALO, d), F32), pltpu.VMEM((tc, d), F32)],
        compiler_params=pltpu.CompilerParams(
            dimension_semantics=("parallel",),
            vmem_limit_bytes=VMEM_LIMIT),
        name="dwconv_ln",
    )(u, u, w_rep, b_rep, ln_g, ln_b, *c_args)
    return outs[0], tuple(outs[1:])


def _linear_residual_kernel(*refs, has_bias):
    if has_bias:
        x_ref, a_ref, w_ref, b_ref, o_ref = refs
    else:
        x_ref, a_ref, w_ref, o_ref = refs
    y = jnp.dot(a_ref[...], w_ref[...], preferred_element_type=F32)
    if has_bias:
        y = y + b_ref[...]
    o_ref[...] = x_ref[...] + y


def _linear_residual(x, a, w, b=None, *, tm=512, tn=D_MODEL):
    m, n = x.shape
    k = a.shape[1]
    in_specs = [
        pl.BlockSpec((tm, tn), lambda i, j: (i, j)),
        pl.BlockSpec((tm, k), lambda i, j: (i, 0)),
        pl.BlockSpec((k, tn), lambda i, j: (0, j)),
    ]
    args = [x, a, w]
    if b is not None:
        in_specs.append(pl.BlockSpec((1, tn), lambda i, j: (0, j)))
        args.append(b)
    return pl.pallas_call(
        functools.partial(_linear_residual_kernel, has_bias=b is not None),
        grid=(m // tm, n // tn),
        in_specs=in_specs,
        out_specs=pl.BlockSpec((tm, tn), lambda i, j: (i, j)),
        out_shape=jax.ShapeDtypeStruct((m, n), F32),
        compiler_params=pltpu.CompilerParams(
            dimension_semantics=("parallel", "arbitrary"),
            vmem_limit_bytes=VMEM_LIMIT),
        name="linear_residual",
    )(*args)


def _mlp_kernel(*refs, tm, final_norm):
    if final_norm:
        x_ref, g_ref, w1_ref, w2_ref, fg_ref, o_ref, xn_ref = refs
    else:
        x_ref, g_ref, w1_ref, w2_ref, o_ref, xn_ref = refs
    f = pl.program_id(1)

    @pl.when(f == 0)
    def _():
        _rmsnorm_into(x_ref, g_ref, xn_ref, tm)
        o_ref[...] = x_ref[...]

    h = jnp.dot(xn_ref[...], w1_ref[...], preferred_element_type=F32)
    h = jnp.square(jnp.maximum(h, 0.0)).astype(BF16)
    o_ref[...] += jnp.dot(h, w2_ref[...], preferred_element_type=F32)

    if final_norm:
        @pl.when(f == pl.num_programs(1) - 1)
        def _():
            _rmsnorm_into(o_ref, fg_ref, o_ref, tm)


def _mlp(x, g, w1, w2, final_g=None, *, tm=1024, tf=1024):
    m, d = x.shape
    dff = w1.shape[1]
    final_norm = final_g is not None
    in_specs = [
        pl.BlockSpec((tm, d), lambda i, f: (i, 0)),
        pl.BlockSpec((1, d), lambda i, f: (0, 0)),
        pl.BlockSpec((d, tf), lambda i, f: (0, f)),
        pl.BlockSpec((tf, d), lambda i, f: (f, 0)),
    ]
    args = [x, g, w1, w2]
    if final_norm:
        in_specs.append(pl.BlockSpec((1, d), lambda i, f: (0, 0)))
        args.append(final_g)
    return pl.pallas_call(
        functools.partial(_mlp_kernel, tm=tm, final_norm=final_norm),
        grid=(m // tm, dff // tf),
        in_specs=in_specs,
        out_specs=pl.BlockSpec((tm, d), lambda i, f: (i, 0)),
        out_shape=jax.ShapeDtypeStruct((m, d), F32),
        scratch_shapes=[pltpu.VMEM((tm, d), BF16)],
        compiler_params=pltpu.CompilerParams(
            dimension_semantics=("parallel", "arbitrary"),
            vmem_limit_bytes=MLP_VMEM_LIMIT),
        name="mlp",
    )(*args)


def _rope_table_kernel(pos_ref, invf_ref, cos_t_ref, sin_t_ref, cos_r_ref, sin_r_ref):
    ang = pos_ref[...].astype(F32) * invf_ref[...]
    c = jnp.cos(ang)
    s = jnp.sin(ang)
    cos_t_ref[...] = c
    sin_t_ref[...] = s
    z = jnp.zeros((LANES - QK_ROPE_DIM, ang.shape[1]), F32)
    cos_r_ref[...] = jnp.concatenate([c, c, z], axis=0).T
    sin_r_ref[...] = jnp.concatenate([s, s, z], axis=0).T


def _rope_tables(positions, *, tr=512):
    half = QK_ROPE_DIM // 2
    m = positions.size
    inv_freq = ROPE_THETA ** (-jnp.arange(0, QK_ROPE_DIM, 2, dtype=F32) / QK_ROPE_DIM)
    return pl.pallas_call(
        _rope_table_kernel,
        grid=(m // tr,),
        in_specs=[pl.BlockSpec((1, tr), lambda i: (0, i)),
                  pl.BlockSpec((half, 1), lambda i: (0, 0))],
        out_specs=[pl.BlockSpec((half, tr), lambda i: (0, i)),
                   pl.BlockSpec((half, tr), lambda i: (0, i)),
                   pl.BlockSpec((tr, LANES), lambda i: (i, 0)),
                   pl.BlockSpec((tr, LANES), lambda i: (i, 0))],
        out_shape=[jax.ShapeDtypeStruct((half, m), F32)] * 2
                  + [jax.ShapeDtypeStruct((m, LANES), F32)] * 2,
        compiler_params=pltpu.CompilerParams(dimension_semantics=("parallel",)),
        name="rope_tables",
    )(positions.reshape(1, m), inv_freq.reshape(half, 1))


def _mla_proj_kernel(x_ref, g_ref, cos_t_ref, sin_t_ref, cos_r_ref, sin_r_ref, win_ref, qg_ref, kvg_ref,
                     wq_ref, wqs_ref, wkn_ref, wv_ref, q_ref, k_ref, v_ref, *, q_scale):
    xn = _rms_rows(x_ref[...], g_ref[...], NORM_EPS).astype(BF16)
    down = jnp.dot(xn, win_ref[...], preferred_element_type=F32)
    c_q = down[:, :Q_LORA_RANK]
    c_kv = down[:, Q_LORA_RANK:Q_LORA_RANK + KV_LORA_RANK]
    base = Q_LORA_RANK + KV_LORA_RANK
    kpe = (down[:, base:base + LANES] * cos_r_ref[...]
           + down[:, base + LANES:base + 2 * LANES] * sin_r_ref[...]).astype(k_ref.dtype)

    cqn = _rms_rows(c_q, qg_ref[...], NORM_EPS).astype(BF16)
    ckvn = _rms_rows(c_kv, kvg_ref[...], NORM_EPS).astype(BF16)

    q = lax.dot_general(wq_ref[...], cqn, _NT, preferred_element_type=F32)
    qs = lax.dot_general(wqs_ref[...], cqn, _NT, preferred_element_type=F32)
    cos2 = jnp.concatenate([cos_t_ref[...]] * 2, axis=0)
    sin2 = jnp.concatenate([sin_t_ref[...]] * 2, axis=0)
    zpad = jnp.zeros((QK_PAD_DIM - QK_HEAD_DIM, q.shape[1]), q_ref.dtype)
    for h in range(N_HEADS):
        src, dst = h * QK_HEAD_DIM, h * QK_PAD_DIM
        q_ref[dst:dst + QK_NOPE_DIM, :] = (q[src:src + QK_NOPE_DIM] * q_scale).astype(q_ref.dtype)
        rot = (q[src + QK_NOPE_DIM:src + QK_HEAD_DIM] * cos2
               + qs[h * QK_ROPE_DIM:(h + 1) * QK_ROPE_DIM] * sin2)
        q_ref[dst + QK_NOPE_DIM:dst + QK_HEAD_DIM, :] = (rot * q_scale).astype(q_ref.dtype)
        q_ref[dst + QK_HEAD_DIM:dst + QK_PAD_DIM, :] = zpad

    kn = jnp.dot(ckvn, wkn_ref[...], preferred_element_type=F32).astype(k_ref.dtype)
    for h in range(N_HEADS):
        dst = h * QK_PAD_DIM
        k_ref[:, dst:dst + QK_NOPE_DIM] = kn[:, h * QK_NOPE_DIM:(h + 1) * QK_NOPE_DIM]
        k_ref[:, dst + QK_NOPE_DIM:dst + QK_PAD_DIM] = kpe
    v = lax.dot_general(wv_ref[...], ckvn, _NT, preferred_element_type=F32).astype(v_ref.dtype)
    ones = jnp.ones((V_PAD_DIM - V_HEAD_DIM, v.shape[1]), v_ref.dtype)
    for h in range(N_HEADS):
        v_ref[h * V_PAD_DIM:h * V_PAD_DIM + V_HEAD_DIM, :] = v[h * V_HEAD_DIM:(h + 1) * V_HEAD_DIM]
        v_ref[h * V_PAD_DIM + V_HEAD_DIM:(h + 1) * V_PAD_DIM, :] = ones


def _mla_proj(x, g, tables, weights, layer, q_g, kv_g, *, q_scale, tm=512):
    m, d = x.shape
    cos_t, sin_t, cos_r, sin_r = tables

    def resident(a):
        return pl.BlockSpec(a.shape, lambda i: (0,) * a.ndim)

    def layer_resident(a):
        return pl.BlockSpec((None,) + a.shape[1:], lambda i: (layer,) + (0,) * (a.ndim - 1))

    def rows(width):
        return pl.BlockSpec((tm, width), lambda i: (i, 0))

    def cols(height):
        return pl.BlockSpec((height, tm), lambda i: (0, i))

    half = QK_ROPE_DIM // 2
    return pl.pallas_call(
        functools.partial(_mla_proj_kernel, q_scale=q_scale),
        grid=(m // tm,),
        in_specs=[rows(d), resident(g), cols(half), cols(half), rows(LANES), rows(LANES),
                  layer_resident(weights[0]), resident(q_g), resident(kv_g)]
                 + [layer_resident(w) for w in weights[1:]],
        out_specs=[cols(N_HEADS * QK_PAD_DIM), rows(N_HEADS * QK_PAD_DIM),
                   pl.BlockSpec((None, N_HEADS * V_PAD_DIM, tm), lambda i: (i, 0, 0))],
        out_shape=[jax.ShapeDtypeStruct((N_HEADS * QK_PAD_DIM, m), BF16),
                   jax.ShapeDtypeStruct((m, N_HEADS * QK_PAD_DIM), BF16),
                   jax.ShapeDtypeStruct((m // tm, N_HEADS * V_PAD_DIM, tm), BF16)],
        compiler_params=pltpu.CompilerParams(
            dimension_semantics=("parallel",),
            vmem_limit_bytes=VMEM_LIMIT),
        name="mla_proj",
    )(x, g, cos_t, sin_t, cos_r, sin_r, weights[0], q_g, kv_g, *weights[1:])


def _attn_kernel(*refs, tq, tk, tv, hb, n_casts):
    q_ref, qn_ref, k_ref, v_ref = refs[:4]
    cast_in = refs[4:4 + n_casts]
    o_ref = refs[4 + n_casts]
    cast_out = refs[5 + n_casts:5 + 2 * n_casts]
    s_scr, m_scr, acc_scr = refs[-3:]
    _emit_casts(cast_in, cast_out)
    qi = pl.program_id(2)
    nv = tk // tv
    m_scr[...] = jnp.full(m_scr.shape, -jnp.inf, F32)
    acc_scr[...] = jnp.zeros(acc_scr.shape, F32)

    def scores(kb, c):
        r = pl.multiple_of(kb * tk, tk)
        return jnp.dot(k_ref[pl.ds(r, tk), c * QK_PAD_DIM:(c + 1) * QK_PAD_DIM],
                       q_ref[c * QK_PAD_DIM:(c + 1) * QK_PAD_DIM, :], preferred_element_type=F32)

    def update(kb, c, s):
        m_prev = m_scr[c]
        m_new = jnp.maximum(m_prev, jnp.max(s, axis=0, keepdims=True))
        alpha = jnp.exp2(m_prev - m_new)
        p = jnp.exp2(s - m_new).astype(BF16)
        m_scr[c] = m_new
        vs = slice(c * V_PAD_DIM, (c + 1) * V_PAD_DIM)
        pv = jnp.dot(v_ref[kb * nv, vs, :], p[:tv], preferred_element_type=F32)
        for t in range(1, nv):
            pv = pv + jnp.dot(v_ref[kb * nv + t, vs, :], p[t * tv:(t + 1) * tv],
                              preferred_element_type=F32)
        acc_scr[c] = alpha * acc_scr[c] + pv

    def advance(kb, cur, nxt):
        for c in range(hb):
            s_scr[nxt, c] = scores(kb + 1, c)
            update(kb, c, s_scr[cur, c])

    @pl.when(qi == 0)
    def _():
        for c in range(hb):
            s_scr[0, c] = scores(0, c)

    def body(t, carry):
        advance(2 * t, 0, 1)
        advance(2 * t + 1, 1, 0)
        return carry

    lax.fori_loop(0, qi // 2, body, 0)

    @pl.when(qi % 2 == 1)
    def _():
        advance(qi - 1, 0, 1)

    k_chunk = lax.broadcasted_iota(jnp.int32, (tk, tq), 0) // CHUNK
    q_chunk = lax.broadcasted_iota(jnp.int32, (tk, tq), 1) // CHUNK
    visible = k_chunk <= q_chunk

    def finish(slot):
        for c in range(hb):
            update(qi, c, jnp.where(visible, s_scr[slot, c], -jnp.inf))
            s_scr[0, c] = jnp.dot(k_ref[0:tk, c * QK_PAD_DIM:(c + 1) * QK_PAD_DIM],
                                  qn_ref[c * QK_PAD_DIM:(c + 1) * QK_PAD_DIM, :],
                                  preferred_element_type=F32)
            acc = acc_scr[c]
            o_ref[:, c * V_HEAD_DIM:(c + 1) * V_HEAD_DIM] = (
                acc[:V_HEAD_DIM] / acc[V_HEAD_DIM:V_HEAD_DIM + 1]).T.astype(o_ref.dtype)

    for parity in range(2):
        pl.when(qi % 2 == parity)(functools.partial(finish, parity))


def _attention(q_t, k, v_t, casts=(), *, batch, seq, tq=512, hb=4):
    tk = tq
    nq = seq // tq
    ng = N_HEADS // hb
    tv = v_t.shape[2]
    grid = (batch, ng, nq)
    c_in, c_out, c_shapes, c_args = _cast_specs(
        casts, batch * ng * nq, lambda b, h, i: (b * ng + h) * nq + i)
    outs = pl.pallas_call(
        functools.partial(_attn_kernel, tq=tq, tk=tk, tv=tv, hb=hb, n_casts=len(casts)),
        grid=grid,
        in_specs=[
            pl.BlockSpec((hb * QK_PAD_DIM, tq), lambda b, h, i: (h, b * nq + i)),
            pl.BlockSpec((hb * QK_PAD_DIM, tq), lambda b, h, i: (h, b * nq + jnp.minimum(i + 1, nq - 1))),
            pl.BlockSpec((seq, hb * QK_PAD_DIM), lambda b, h, i: (b, h)),
            pl.BlockSpec((seq // tv, hb * V_PAD_DIM, tv), lambda b, h, i: (b, h, 0)),
        ] + c_in,
        out_specs=[pl.BlockSpec((tq, hb * V_HEAD_DIM), lambda b, h, i: (b * nq + i, h))] + c_out,
        out_shape=[jax.ShapeDtypeStruct((batch * seq, N_HEADS * V_HEAD_DIM), BF16)] + c_shapes,
        scratch_shapes=[pltpu.VMEM((2, hb, tk, tq), F32), pltpu.VMEM((hb, 1, tq), F32),
                        pltpu.VMEM((hb, V_PAD_DIM, tq), F32)],
        compiler_params=pltpu.CompilerParams(
            dimension_semantics=("parallel", "parallel", "arbitrary"),
            vmem_limit_bytes=VMEM_LIMIT),
        name="mla_attention",
    )(q_t, q_t, k, v_t, *c_args)
    return outs[0], tuple(outs[1:])


def _prep_mla_weights(w_in, w_q_up, w_kv_up):
    w_in, w_q_up, w_kv_up = (w.astype(BF16) for w in (w_in, w_q_up, w_kv_up))
    nl, d, _ = w_in.shape
    half = QK_ROPE_DIM // 2
    base = Q_LORA_RANK + KV_LORA_RANK
    pad = jnp.zeros((nl, d, LANES - QK_ROPE_DIM), w_in.dtype)
    k_pe = w_in[:, :, base:]
    w_in_p = jnp.concatenate(
        [w_in[:, :, :base], k_pe, pad, -k_pe[:, :, half:], k_pe[:, :, :half], pad], axis=2)

    r = w_q_up.shape[1]
    pe = w_q_up.reshape(nl, r, N_HEADS, QK_HEAD_DIM)[:, :, :, QK_NOPE_DIM:]
    wq_t = jnp.swapaxes(w_q_up, 1, 2)
    wqs_t = jnp.swapaxes(
        jnp.concatenate([-pe[..., half:], pe[..., :half]], axis=3).reshape(nl, r, -1), 1, 2)

    rk = w_kv_up.shape[1]
    wkv4 = w_kv_up.reshape(nl, rk, N_HEADS, QK_NOPE_DIM + V_HEAD_DIM)
    wkn_p = wkv4[..., :QK_NOPE_DIM].reshape(nl, rk, N_HEADS * QK_NOPE_DIM)
    wv_t = jnp.swapaxes(wkv4[..., QK_NOPE_DIM:].reshape(nl, rk, N_HEADS * V_HEAD_DIM), 1, 2)
    return w_in_p, wq_t, wqs_t, wkn_p, wv_t


def kernel(x, positions, norm_mixer_g, norm_mlp_g, conv_w_pw1, conv_b_pw1, conv_w_dw, conv_b_dw, conv_ln_g, conv_ln_b, conv_w_pw2, conv_b_pw2, mla_w_in, mla_q_norm_g, mla_kv_norm_g, mla_w_q_up, mla_w_kv_up, mla_w_o, mlp_w1, mlp_w2, final_norm_g):
    batch, seq, d = x.shape
    m = batch * seq
    xs = x.reshape(m, d)

    def row(v):
        return v.reshape(1, -1)

    mla_weights = _prep_mla_weights(mla_w_in, mla_w_q_up, mla_w_kv_up)
    tables = _rope_tables(positions)
    q_scale = QK_HEAD_DIM ** -0.5 * math.log2(math.e)

    w_pw1 = conv_w_pw1[0].astype(BF16)
    w_o = None
    for layer in range(DEPTH):
        j = layer // N_MIXERS
        g_mix = row(norm_mixer_g[layer])
        if layer % N_MIXERS == 0:
            u = _pw1_glu(xs, g_mix, w_pw1, row(conv_b_pw1[j]))
            a, (w1, w2, w_pw2, w_o) = _dwconv_ln(
                u, conv_w_dw[j], row(conv_b_dw[j]), row(conv_ln_g[j]), row(conv_ln_b[j]),
                ((mlp_w1, layer), (mlp_w2, layer), (conv_w_pw2, j), (mla_w_o, j)), seq=seq)
            xs = _linear_residual(xs, a, w_pw2, row(conv_b_pw2[j]))
        else:
            q, k, v = _mla_proj(xs, g_mix, tables, mla_weights, j, row(mla_q_norm_g[j]),
                                row(mla_kv_norm_g[j]), q_scale=q_scale)
            casts = [(mlp_w1, layer), (mlp_w2, layer)]
            if j + 1 < conv_w_pw1.shape[0]:
                casts.append((conv_w_pw1, j + 1))
            o, (w1, w2, *rest) = _attention(q, k, v, tuple(casts), batch=batch, seq=seq)
            w_pw1 = rest[0] if rest else None
            xs = _linear_residual(xs, o, w_o)
        final_g = row(final_norm_g) if layer == DEPTH - 1 else None
        xs = _mlp(xs, row(norm_mlp_g[layer]), w1, w2, final_g)
    return xs.reshape(batch, seq, d)
```

```python
import functools
import math

import jax
import jax.numpy as jnp
from jax import lax
from jax.experimental import pallas as pl
from jax.experimental.pallas import tpu as pltpu

D_MODEL = 2048
DEPTH = 4
CHUNK = 64
N_MIXERS = 2
CONV_WIDTH = 31
N_HEADS = 16
QK_NOPE_DIM = 128
QK_ROPE_DIM = 64
V_HEAD_DIM = 128
Q_LORA_RANK = 512
KV_LORA_RANK = 512
D_FF = 4 * D_MODEL
ROPE_THETA = 10000.0
NORM_EPS = 1e-6
LN_EPS = 1e-5

SUBLANES = 8
LANES = 128
QK_HEAD_DIM = QK_NOPE_DIM + QK_ROPE_DIM
V_PAD_DIM = V_HEAD_DIM + 16
QK_PAD_DIM = 2 * LANES
HALO = 32
CONV_ROWS = 32
CONV_COLS = 512
VMEM_LIMIT = 56 * 1024 * 1024
MLP_VMEM_LIMIT = 60 * 1024 * 1024

F32 = jnp.float32
BF16 = jnp.bfloat16

_TT = (((0,), (1,)), ((), ()))


def _rms_rows(x, g, eps):
    ms = jnp.mean(x * x, axis=-1, keepdims=True)
    return x * lax.rsqrt(ms + eps) * g


def _rmsnorm_into(x_ref, g_ref, out_ref, rows, chunk=256):
    g = g_ref[...]

    def body(c, carry):
        r = pl.multiple_of(c * chunk, chunk)
        x = x_ref[pl.ds(r, chunk), :]
        out_ref[pl.ds(r, chunk), :] = _rms_rows(x, g, NORM_EPS).astype(out_ref.dtype)
        return carry

    lax.fori_loop(0, rows // chunk, body, 0)


def _cast_specs(casts, n_steps, step_of):
    in_specs, out_specs, out_shapes, args = [], [], [], []
    for w_all, layer in casts:
        _, r, c = w_all.shape
        blk = (r // n_steps, c)
        in_specs.append(pl.BlockSpec((None,) + blk, lambda *ids, layer=layer: (layer, step_of(*ids), 0)))
        out_specs.append(pl.BlockSpec(blk, lambda *ids: (step_of(*ids), 0)))
        out_shapes.append(jax.ShapeDtypeStruct((r, c), BF16))
        args.append(w_all)
    return in_specs, out_specs, out_shapes, args


def _emit_casts(cast_in, cast_out):
    for src, dst in zip(cast_in, cast_out):
        dst[...] = src[...].astype(dst.dtype)


def _pw1_glu_kernel(x_ref, g_ref, wa_ref, wg_ref, ba_ref, bg_ref, o_ref, xn_ref, *, tm):
    @pl.when(pl.program_id(1) == 0)
    def _():
        _rmsnorm_into(x_ref, g_ref, xn_ref, tm)

    xn = xn_ref[...]
    tn = o_ref.shape[1]
    for cs in (slice(0, tn // 2), slice(tn // 2, tn)):
        a = jnp.dot(xn, wa_ref[:, cs], preferred_element_type=F32) + ba_ref[:, cs]
        gate = jnp.dot(xn, wg_ref[:, cs], preferred_element_type=F32) + bg_ref[:, cs]
        o_ref[:, cs] = a * jax.nn.sigmoid(gate)


def _pw1_glu(x, g, w, b, *, tm=1024, tn=1024):
    m, d = x.shape
    nj = d // tn
    return pl.pallas_call(
        functools.partial(_pw1_glu_kernel, tm=tm),
        grid=(m // tm, nj),
        in_specs=[
            pl.BlockSpec((tm, d), lambda i, j: (i, 0)),
            pl.BlockSpec((1, d), lambda i, j: (0, 0)),
            pl.BlockSpec((d, tn), lambda i, j: (0, j)),
            pl.BlockSpec((d, tn), lambda i, j: (0, j + nj)),
            pl.BlockSpec((1, tn), lambda i, j: (0, j)),
            pl.BlockSpec((1, tn), lambda i, j: (0, j + nj)),
        ],
        out_specs=pl.BlockSpec((tm, tn), lambda i, j: (i, j)),
        out_shape=jax.ShapeDtypeStruct((m, d), F32),
        scratch_shapes=[pltpu.VMEM((tm, d), BF16)],
        compiler_params=pltpu.CompilerParams(
            dimension_semantics=("parallel", "arbitrary"),
            vmem_limit_bytes=VMEM_LIMIT),
        name="pw1_glu",
    )(x, g, w, w, b, b)


def _dwconv_ln_kernel(*refs, tc, tiles_per_seq, n_casts):
    u_ref, halo_ref, w_ref, b_ref, lg_ref, lb_ref = refs[:6]
    cast_in = refs[6:6 + n_casts]
    o_ref = refs[6 + n_casts]
    cast_out = refs[7 + n_casts:7 + 2 * n_casts]
    sh_ref, conv_ref = refs[-2:]
    _emit_casts(cast_in, cast_out)
    d = u_ref.shape[1]
    first = (pl.program_id(0) % tiles_per_seq) == 0
    halo = halo_ref[...]
    sh_ref[0, 0:HALO, :] = jnp.where(first, jnp.zeros_like(halo), halo)
    sh_ref[0, HALO:, :] = u_ref[...]
    n = tc + HALO - SUBLANES
    for s in range(1, SUBLANES):
        sh_ref[s, 0:n, :] = sh_ref[0, s:s + n, :]

    off = HALO - (CONV_WIDTH - 1)

    def row_body(ri, carry):
        r = pl.multiple_of(ri * CONV_ROWS, CONV_ROWS)
        groups = range(0, CONV_ROWS, SUBLANES)
        for c in range(0, d, CONV_COLS):
            cs = slice(c, c + CONV_COLS)
            bias = b_ref[:, cs]
            accs = [bias for _ in groups]
            for k in range(CONV_WIDTH):
                a, s = divmod(k + off, SUBLANES)
                wk = w_ref[k, :, cs]
                accs = [acc + sh_ref[s, pl.ds(r + SUBLANES * a + g, SUBLANES), cs] * wk
                        for acc, g in zip(accs, groups)]
            for acc, g in zip(accs, groups):
                conv_ref[pl.ds(r + g, SUBLANES), cs] = acc
        return carry

    lax.fori_loop(0, tc // CONV_ROWS, row_body, 0)

    y = conv_ref[...]
    mu = jnp.mean(y, axis=-1, keepdims=True)
    yc = y - mu
    var = jnp.mean(yc * yc, axis=-1, keepdims=True)
    z = yc * lax.rsqrt(var + LN_EPS) * lg_ref[...] + lb_ref[...]
    o_ref[...] = (z * jax.nn.sigmoid(z)).astype(o_ref.dtype)


def _dwconv_ln(u, w_dw, b_dw, ln_g, ln_b, casts=(), *, seq, tc=256):
    m, d = u.shape
    halo_blocks = tc // HALO
    grid = (m // tc,)
    c_in, c_out, c_shapes, c_args = _cast_specs(casts, grid[0], lambda i: i)
    w_rep = jnp.broadcast_to(w_dw[:, None, :], (CONV_WIDTH, SUBLANES, d))
    b_rep = jnp.broadcast_to(b_dw, (SUBLANES, d))
    outs = pl.pallas_call(
        functools.partial(_dwconv_ln_kernel, tc=tc, tiles_per_seq=seq // tc, n_casts=len(casts)),
        grid=grid,
        in_specs=[
            pl.BlockSpec((tc, d), lambda i: (i, 0)),
            pl.BlockSpec((HALO, d), lambda i: (jnp.maximum(i * halo_blocks - 1, 0), 0)),
            pl.BlockSpec((CONV_WIDTH, SUBLANES, d), lambda i: (0, 0, 0)),
            pl.BlockSpec((SUBLANES, d), lambda i: (0, 0)),
            pl.BlockSpec((1, d), lambda i: (0, 0)),
            pl.BlockSpec((1, d), lambda i: (0, 0)),
        ] + c_in,
        out_specs=[pl.BlockSpec((tc, d), lambda i: (i, 0))] + c_out,
        out_shape=[jax.ShapeDtypeStruct((m, d), BF16)] + c_shapes,
        scratch_shapes=[pltpu.VMEM((SUBLANES, tc + HALO, d), F32), pltpu.VMEM((tc, d), F32)],
        compiler_params=pltpu.CompilerParams(
            dimension_semantics=("parallel",),
            vmem_limit_bytes=VMEM_LIMIT),
        name="dwconv_ln",
    )(u, u, w_rep, b_rep, ln_g, ln_b, *c_args)
    return outs[0], tuple(outs[1:])


def _linear_residual_kernel(*refs, has_bias):
    if has_bias:
        x_ref, a_ref, w_ref, b_ref, o_ref = refs
    else:
        x_ref, a_ref, w_ref, o_ref = refs
    y = jnp.dot(a_ref[...], w_ref[...], preferred_element_type=F32)
    if has_bias:
        y = y + b_ref[...]
    o_ref[...] = x_ref[...] + y


def _linear_residual(x, a, w, b=None, *, tm=512, tn=D_MODEL):
    m, n = x.shape
    k = a.shape[1]
    in_specs = [
        pl.BlockSpec((tm, tn), lambda i, j: (i, j)),
        pl.BlockSpec((tm, k), lambda i, j: (i, 0)),
        pl.BlockSpec((k, tn), lambda i, j: (0, j)),
    ]
    args = [x, a, w]
    if b is not None:
        in_specs.append(pl.BlockSpec((1, tn), lambda i, j: (0, j)))
        args.append(b)
    return pl.pallas_call(
        functools.partial(_linear_residual_kernel, has_bias=b is not None),
        grid=(m // tm, n // tn),
        in_specs=in_specs,
        out_specs=pl.BlockSpec((tm, tn), lambda i, j: (i, j)),
        out_shape=jax.ShapeDtypeStruct((m, n), F32),
        compiler_params=pltpu.CompilerParams(
            dimension_semantics=("parallel", "arbitrary"),
            vmem_limit_bytes=VMEM_LIMIT),
        name="linear_residual",
    )(*args)


def _mlp_kernel(*refs, tm, final_norm):
    if final_norm:
        x_ref, g_ref, w1_ref, w2_ref, fg_ref, o_ref, xn_ref = refs
    else:
        x_ref, g_ref, w1_ref, w2_ref, o_ref, xn_ref = refs
    f = pl.program_id(1)

    @pl.when(f == 0)
    def _():
        _rmsnorm_into(x_ref, g_ref, xn_ref, tm)
        o_ref[...] = x_ref[...]

    h = jnp.dot(xn_ref[...], w1_ref[...], preferred_element_type=F32)
    h = jnp.square(jnp.maximum(h, 0.0)).astype(BF16)
    o_ref[...] += jnp.dot(h, w2_ref[...], preferred_element_type=F32)

    if final_norm:
        @pl.when(f == pl.num_programs(1) - 1)
        def _():
            _rmsnorm_into(o_ref, fg_ref, o_ref, tm)


def _mlp(x, g, w1, w2, final_g=None, *, tm=1024, tf=1024):
    m, d = x.shape
    dff = w1.shape[1]
    final_norm = final_g is not None
    in_specs = [
        pl.BlockSpec((tm, d), lambda i, f: (i, 0)),
        pl.BlockSpec((1, d), lambda i, f: (0, 0)),
        pl.BlockSpec((d, tf), lambda i, f: (0, f)),
        pl.BlockSpec((tf, d), lambda i, f: (f, 0)),
    ]
    args = [x, g, w1, w2]
    if final_norm:
        in_specs.append(pl.BlockSpec((1, d), lambda i, f: (0, 0)))
        args.append(final_g)
    return pl.pallas_call(
        functools.partial(_mlp_kernel, tm=tm, final_norm=final_norm),
        grid=(m // tm, dff // tf),
        in_specs=in_specs,
        out_specs=pl.BlockSpec((tm, d), lambda i, f: (i, 0)),
        out_shape=jax.ShapeDtypeStruct((m, d), F32),
        scratch_shapes=[pltpu.VMEM((tm, d), BF16)],
        compiler_params=pltpu.CompilerParams(
            dimension_semantics=("parallel", "arbitrary"),
            vmem_limit_bytes=MLP_VMEM_LIMIT),
        name="mlp",
    )(*args)


def _rope_table_kernel(pos_ref, invf_ref, cos_t_ref, sin_t_ref, cos_r_ref, sin_r_ref):
    ang = pos_ref[...].astype(F32) * invf_ref[...]
    c = jnp.cos(ang)
    s = jnp.sin(ang)
    cos_t_ref[...] = c
    sin_t_ref[...] = s
    z = jnp.zeros((LANES - QK_ROPE_DIM, ang.shape[1]), F32)
    cos_r_ref[...] = jnp.concatenate([c, c, z], axis=0).T
    sin_r_ref[...] = jnp.concatenate([s, s, z], axis=0).T


def _rope_tables(positions, *, tr=2048):
    half = QK_ROPE_DIM // 2
    m = positions.size
    inv_freq = ROPE_THETA ** (-jnp.arange(0, QK_ROPE_DIM, 2, dtype=F32) / QK_ROPE_DIM)
    return pl.pallas_call(
        _rope_table_kernel,
        grid=(m // tr,),
        in_specs=[pl.BlockSpec((1, tr), lambda i: (0, i)),
                  pl.BlockSpec((half, 1), lambda i: (0, 0))],
        out_specs=[pl.BlockSpec((half, tr), lambda i: (0, i)),
                   pl.BlockSpec((half, tr), lambda i: (0, i)),
                   pl.BlockSpec((tr, LANES), lambda i: (i, 0)),
                   pl.BlockSpec((tr, LANES), lambda i: (i, 0))],
        out_shape=[jax.ShapeDtypeStruct((half, m), F32)] * 2
                  + [jax.ShapeDtypeStruct((m, LANES), F32)] * 2,
        compiler_params=pltpu.CompilerParams(dimension_semantics=("parallel",)),
        name="rope_tables",
    )(positions.reshape(1, m), inv_freq.reshape(half, 1))


def _mla_proj_kernel(x_ref, g_ref, cos_t_ref, sin_t_ref, cos_r_ref, sin_r_ref, win_ref, qg_ref, kvg_ref,
                     wkpe_ref, wq_ref, wqs_ref, wkn_ref, wv_ref, q_ref, k_ref, v_ref, *, q_scale):
    xn = _rms_rows(x_ref[...], g_ref[...], NORM_EPS).astype(BF16)
    base = Q_LORA_RANK + KV_LORA_RANK
    down = jnp.dot(xn, win_ref[:, :base], preferred_element_type=F32)
    c_q = down[:, :Q_LORA_RANK]
    c_kv = down[:, Q_LORA_RANK:]
    kp = jnp.dot(xn, wkpe_ref[...], preferred_element_type=F32)
    kpe = (kp[:, :LANES] * cos_r_ref[...] + kp[:, LANES:] * sin_r_ref[...]).astype(k_ref.dtype)

    cqn = _rms_rows(c_q, qg_ref[...], NORM_EPS).astype(BF16)
    ckvn = _rms_rows(c_kv, kvg_ref[...], NORM_EPS).astype(BF16)

    q = lax.dot_general(wq_ref[...], cqn, _TT, preferred_element_type=F32)
    qs = lax.dot_general(wqs_ref[...], cqn, _TT, preferred_element_type=F32)
    cos2 = jnp.concatenate([cos_t_ref[...]] * 2, axis=0)
    sin2 = jnp.concatenate([sin_t_ref[...]] * 2, axis=0)
    zpad = jnp.zeros((QK_PAD_DIM - QK_HEAD_DIM, q.shape[1]), q_ref.dtype)
    for h in range(N_HEADS):
        src, dst = h * QK_HEAD_DIM, h * QK_PAD_DIM
        q_ref[dst:dst + QK_NOPE_DIM, :] = (q[src:src + QK_NOPE_DIM] * q_scale).astype(q_ref.dtype)
        rot = (q[src + QK_NOPE_DIM:src + QK_HEAD_DIM] * cos2
               + qs[h * QK_ROPE_DIM:(h + 1) * QK_ROPE_DIM] * sin2)
        q_ref[dst + QK_NOPE_DIM:dst + QK_HEAD_DIM, :] = (rot * q_scale).astype(q_ref.dtype)
        q_ref[dst + QK_HEAD_DIM:dst + QK_PAD_DIM, :] = zpad

    kn = jnp.dot(ckvn, wkn_ref[...], preferred_element_type=F32).astype(k_ref.dtype)
    for h in range(N_HEADS):
        dst = h * QK_PAD_DIM
        k_ref[:, dst:dst + QK_NOPE_DIM] = kn[:, h * QK_NOPE_DIM:(h + 1) * QK_NOPE_DIM]
        k_ref[:, dst + QK_NOPE_DIM:dst + QK_PAD_DIM] = kpe
    v = lax.dot_general(wv_ref[...], ckvn, _TT, preferred_element_type=F32).astype(v_ref.dtype)
    ones = jnp.ones((V_PAD_DIM - V_HEAD_DIM, v.shape[1]), v_ref.dtype)
    for h in range(N_HEADS):
        v_ref[h * V_PAD_DIM:h * V_PAD_DIM + V_HEAD_DIM, :] = v[h * V_HEAD_DIM:(h + 1) * V_HEAD_DIM]
        v_ref[h * V_PAD_DIM + V_HEAD_DIM:(h + 1) * V_PAD_DIM, :] = ones


def _mla_proj(x, g, tables, weights, layer, q_g, kv_g, *, q_scale, tm=512):
    m, d = x.shape
    cos_t, sin_t, cos_r, sin_r = tables

    def resident(a):
        return pl.BlockSpec(a.shape, lambda i: (0,) * a.ndim)

    def layer_resident(a):
        return pl.BlockSpec((None,) + a.shape[1:], lambda i: (layer,) + (0,) * (a.ndim - 1))

    def rows(width):
        return pl.BlockSpec((tm, width), lambda i: (i, 0))

    def cols(height):
        return pl.BlockSpec((height, tm), lambda i: (0, i))

    half = QK_ROPE_DIM // 2
    return pl.pallas_call(
        functools.partial(_mla_proj_kernel, q_scale=q_scale),
        grid=(m // tm,),
        in_specs=[rows(d), resident(g), cols(half), cols(half), rows(LANES), rows(LANES),
                  layer_resident(weights[0]), resident(q_g), resident(kv_g)]
                 + [layer_resident(w) for w in weights[1:]],
        out_specs=[cols(N_HEADS * QK_PAD_DIM), rows(N_HEADS * QK_PAD_DIM),
                   pl.BlockSpec((None, N_HEADS * V_PAD_DIM, tm), lambda i: (i, 0, 0))],
        out_shape=[jax.ShapeDtypeStruct((N_HEADS * QK_PAD_DIM, m), BF16),
                   jax.ShapeDtypeStruct((m, N_HEADS * QK_PAD_DIM), BF16),
                   jax.ShapeDtypeStruct((m // tm, N_HEADS * V_PAD_DIM, tm), BF16)],
        compiler_params=pltpu.CompilerParams(
            dimension_semantics=("parallel",),
            vmem_limit_bytes=VMEM_LIMIT),
        name="mla_proj",
    )(x, g, cos_t, sin_t, cos_r, sin_r, weights[0], q_g, kv_g, *weights[1:])


def _attn_kernel(*refs, tq, tk, tv, hb, n_casts):
    q_ref, qn_ref, k_ref, v_ref = refs[:4]
    cast_in = refs[4:4 + n_casts]
    o_ref = refs[4 + n_casts]
    cast_out = refs[5 + n_casts:5 + 2 * n_casts]
    s_scr, m_scr, acc_scr = refs[-3:]
    _emit_casts(cast_in, cast_out)
    qi = pl.program_id(2)
    nv = tk // tv
    m_scr[...] = jnp.full(m_scr.shape, -jnp.inf, F32)
    acc_scr[...] = jnp.zeros(acc_scr.shape, F32)

    def scores(kb, c):
        r = pl.multiple_of(kb * tk, tk)
        return jnp.dot(k_ref[pl.ds(r, tk), c * QK_PAD_DIM:(c + 1) * QK_PAD_DIM],
                       q_ref[c * QK_PAD_DIM:(c + 1) * QK_PAD_DIM, :], preferred_element_type=F32)

    def update(kb, c, s):
        m_prev = m_scr[c]
        m_new = jnp.maximum(m_prev, jnp.max(s, axis=0, keepdims=True))
        alpha = jnp.exp2(m_prev - m_new)
        p = jnp.exp2(s - m_new).astype(BF16)
        m_scr[c] = m_new
        vs = slice(c * V_PAD_DIM, (c + 1) * V_PAD_DIM)
        pv = jnp.dot(v_ref[kb * nv, vs, :], p[:tv], preferred_element_type=F32)
        for t in range(1, nv):
            pv = pv + jnp.dot(v_ref[kb * nv + t, vs, :], p[t * tv:(t + 1) * tv],
                              preferred_element_type=F32)
        acc_scr[c] = alpha * acc_scr[c] + pv

    def advance(kb, cur, nxt):
        for c in range(hb):
            s_scr[nxt, c] = scores(kb + 1, c)
            update(kb, c, s_scr[cur, c])

    @pl.when(qi == 0)
    def _():
        for c in range(hb):
            s_scr[0, c] = scores(0, c)

    def body(t, carry):
        advance(2 * t, 0, 1)
        advance(2 * t + 1, 1, 0)
        return carry

    lax.fori_loop(0, qi // 2, body, 0)

    @pl.when(qi % 2 == 1)
    def _():
        advance(qi - 1, 0, 1)

    k_chunk = lax.broadcasted_iota(jnp.int32, (tk, tq), 0) // CHUNK
    q_chunk = lax.broadcasted_iota(jnp.int32, (tk, tq), 1) // CHUNK
    visible = k_chunk <= q_chunk

    def finish(slot):
        for c in range(hb):
            update(qi, c, jnp.where(visible, s_scr[slot, c], -jnp.inf))
            s_scr[0, c] = jnp.dot(k_ref[0:tk, c * QK_PAD_DIM:(c + 1) * QK_PAD_DIM],
                                  qn_ref[c * QK_PAD_DIM:(c + 1) * QK_PAD_DIM, :],
                                  preferred_element_type=F32)
            acc = acc_scr[c]
            o_ref[:, c * V_HEAD_DIM:(c + 1) * V_HEAD_DIM] = (
                acc[:V_HEAD_DIM] / acc[V_HEAD_DIM:V_HEAD_DIM + 1]).T.astype(o_ref.dtype)

    for parity in range(2):
        pl.when(qi % 2 == parity)(functools.partial(finish, parity))


def _attention(q_t, k, v_t, casts=(), *, batch, seq, tq=512, hb=4):
    tk = tq
    nq = seq // tq
    ng = N_HEADS // hb
    tv = v_t.shape[2]
    grid = (batch, ng, nq)
    c_in, c_out, c_shapes, c_args = _cast_specs(
        casts, batch * ng * nq, lambda b, h, i: (b * ng + h) * nq + i)
    outs = pl.pallas_call(
        functools.partial(_attn_kernel, tq=tq, tk=tk, tv=tv, hb=hb, n_casts=len(casts)),
        grid=grid,
        in_specs=[
            pl.BlockSpec((hb * QK_PAD_DIM, tq), lambda b, h, i: (h, b * nq + i)),
            pl.BlockSpec((hb * QK_PAD_DIM, tq), lambda b, h, i: (h, b * nq + jnp.minimum(i + 1, nq - 1))),
            pl.BlockSpec((seq, hb * QK_PAD_DIM), lambda b, h, i: (b, h)),
            pl.BlockSpec((seq // tv, hb * V_PAD_DIM, tv), lambda b, h, i: (b, h, 0)),
        ] + c_in,
        out_specs=[pl.BlockSpec((tq, hb * V_HEAD_DIM), lambda b, h, i: (b * nq + i, h))] + c_out,
        out_shape=[jax.ShapeDtypeStruct((batch * seq, N_HEADS * V_HEAD_DIM), BF16)] + c_shapes,
        scratch_shapes=[pltpu.VMEM((2, hb, tk, tq), F32), pltpu.VMEM((hb, 1, tq), F32),
                        pltpu.VMEM((hb, V_PAD_DIM, tq), F32)],
        compiler_params=pltpu.CompilerParams(
            dimension_semantics=("parallel", "parallel", "arbitrary"),
            vmem_limit_bytes=VMEM_LIMIT),
        name="mla_attention",
    )(q_t, q_t, k, v_t, *c_args)
    return outs[0], tuple(outs[1:])


def _prep_mla_weights(w_in, w_q_up, w_kv_up):
    w_in, w_q_up, w_kv_up = (w.astype(BF16) for w in (w_in, w_q_up, w_kv_up))
    nl, d, _ = w_in.shape
    half = QK_ROPE_DIM // 2
    base = Q_LORA_RANK + KV_LORA_RANK
    pad = jnp.zeros((nl, d, LANES - QK_ROPE_DIM), w_in.dtype)
    k_pe = w_in[:, :, base:]
    w_kpe = jnp.concatenate([k_pe, pad, -k_pe[:, :, half:], k_pe[:, :, :half], pad], axis=2)

    r = w_q_up.shape[1]
    pe = w_q_up.reshape(nl, r, N_HEADS, QK_HEAD_DIM)[:, :, :, QK_NOPE_DIM:]
    wq_swap = jnp.concatenate([-pe[..., half:], pe[..., :half]], axis=3).reshape(nl, r, -1)

    rk = w_kv_up.shape[1]
    wkv4 = w_kv_up.reshape(nl, rk, N_HEADS, QK_NOPE_DIM + V_HEAD_DIM)
    wkn = wkv4[..., :QK_NOPE_DIM].reshape(nl, rk, N_HEADS * QK_NOPE_DIM)
    wv = wkv4[..., QK_NOPE_DIM:].reshape(nl, rk, N_HEADS * V_HEAD_DIM)
    return w_in, w_kpe, w_q_up, wq_swap, wkn, wv


def kernel(x, positions, norm_mixer_g, norm_mlp_g, conv_w_pw1, conv_b_pw1, conv_w_dw, conv_b_dw, conv_ln_g, conv_ln_b, conv_w_pw2, conv_b_pw2, mla_w_in, mla_q_norm_g, mla_kv_norm_g, mla_w_q_up, mla_w_kv_up, mla_w_o, mlp_w1, mlp_w2, final_norm_g):
    batch, seq, d = x.shape
    m = batch * seq
    xs = x.reshape(m, d)

    def row(v):
        return v.reshape(1, -1)

    mla_weights = _prep_mla_weights(mla_w_in, mla_w_q_up, mla_w_kv_up)
    tables = _rope_tables(positions)
    q_scale = QK_HEAD_DIM ** -0.5 * math.log2(math.e)

    w_pw1 = conv_w_pw1[0].astype(BF16)
    w_o = None
    for layer in range(DEPTH):
        j = layer // N_MIXERS
        g_mix = row(norm_mixer_g[layer])
        if layer % N_MIXERS == 0:
            u = _pw1_glu(xs, g_mix, w_pw1, row(conv_b_pw1[j]))
            a, (w1, w2, w_pw2, w_o) = _dwconv_ln(
                u, conv_w_dw[j], row(conv_b_dw[j]), row(conv_ln_g[j]), row(conv_ln_b[j]),
                ((mlp_w1, layer), (mlp_w2, layer), (conv_w_pw2, j), (mla_w_o, j)), seq=seq)
            xs = _linear_residual(xs, a, w_pw2, row(conv_b_pw2[j]))
        else:
            q, k, v = _mla_proj(xs, g_mix, tables, mla_weights, j, row(mla_q_norm_g[j]),
                                row(mla_kv_norm_g[j]), q_scale=q_scale)
            casts = [(mlp_w1, layer), (mlp_w2, layer)]
            if j + 1 < conv_w_pw1.shape[0]:
                casts.append((conv_w_pw1, j + 1))
            o, (w1, w2, *rest) = _attention(q, k, v, tuple(casts), batch=batch, seq=seq)
            w_pw1 = rest[0] if rest else None
            xs = _linear_residual(xs, o, w_o)
        final_g = row(final_norm_g) if layer == DEPTH - 1 else None
        xs = _mlp(xs, row(norm_mlp_g[layer]), w1, w2, final_g)
    return xs.reshape(batch, seq, d)
```

```python
import functools
import math

import jax
import jax.numpy as jnp
from jax import lax
from jax.experimental import pallas as pl
from jax.experimental.pallas import tpu as pltpu

D_MODEL = 2048
DEPTH = 4
CHUNK = 64
N_MIXERS = 2
CONV_WIDTH = 31
N_HEADS = 16
QK_NOPE_DIM = 128
QK_ROPE_DIM = 64
V_HEAD_DIM = 128
Q_LORA_RANK = 512
KV_LORA_RANK = 512
D_FF = 4 * D_MODEL
ROPE_THETA = 10000.0
NORM_EPS = 1e-6
LN_EPS = 1e-5

SUBLANES = 8
LANES = 128
QK_HEAD_DIM = QK_NOPE_DIM + QK_ROPE_DIM
V_PAD_DIM = V_HEAD_DIM + 16
QK_PAD_DIM = 2 * LANES
HALO = 32
CONV_ROWS = 64
CONV_COLS = 512
VMEM_LIMIT = 56 * 1024 * 1024
BIG_VMEM_LIMIT = 60 * 1024 * 1024

F32 = jnp.float32
BF16 = jnp.bfloat16

_TT = (((0,), (1,)), ((), ()))


def _rms_rows(x, g, eps):
    ms = jnp.mean(x * x, axis=-1, keepdims=True)
    return x * lax.rsqrt(ms + eps) * g


def _rmsnorm_into(x_ref, g_ref, out_ref, rows, chunk=256):
    g = g_ref[...]

    def body(c, carry):
        r = pl.multiple_of(c * chunk, chunk)
        x = x_ref[pl.ds(r, chunk), :]
        out_ref[pl.ds(r, chunk), :] = _rms_rows(x, g, NORM_EPS).astype(out_ref.dtype)
        return carry

    lax.fori_loop(0, rows // chunk, body, 0)


def _cast_specs(casts, n_steps, step_of):
    in_specs, out_specs, out_shapes, args = [], [], [], []
    for w_all, layer in casts:
        _, r, c = w_all.shape
        blk = (r // n_steps, c)
        in_specs.append(pl.BlockSpec((None,) + blk, lambda *ids, layer=layer: (layer, step_of(*ids), 0)))
        out_specs.append(pl.BlockSpec(blk, lambda *ids: (step_of(*ids), 0)))
        out_shapes.append(jax.ShapeDtypeStruct((r, c), BF16))
        args.append(w_all)
    return in_specs, out_specs, out_shapes, args


def _emit_casts(cast_in, cast_out):
    for src, dst in zip(cast_in, cast_out):
        dst[...] = src[...].astype(dst.dtype)


def _pw1_glu_kernel(x_ref, g_ref, wa_ref, wg_ref, ba_ref, bg_ref, o_ref, xn_ref, *, tm):
    @pl.when(pl.program_id(1) == 0)
    def _():
        _rmsnorm_into(x_ref, g_ref, xn_ref, tm)

    xn = xn_ref[...]
    tn = o_ref.shape[1]
    for cs in (slice(0, tn // 2), slice(tn // 2, tn)):
        a = jnp.dot(xn, wa_ref[:, cs], preferred_element_type=F32) + ba_ref[:, cs]
        gate = jnp.dot(xn, wg_ref[:, cs], preferred_element_type=F32) + bg_ref[:, cs]
        o_ref[:, cs] = a * jax.nn.sigmoid(gate)


def _pw1_glu(x, g, w, b, *, tm=1024, tn=1024):
    m, d = x.shape
    nj = d // tn
    return pl.pallas_call(
        functools.partial(_pw1_glu_kernel, tm=tm),
        grid=(m // tm, nj),
        in_specs=[
            pl.BlockSpec((tm, d), lambda i, j: (i, 0)),
            pl.BlockSpec((1, d), lambda i, j: (0, 0)),
            pl.BlockSpec((d, tn), lambda i, j: (0, j)),
            pl.BlockSpec((d, tn), lambda i, j: (0, j + nj)),
            pl.BlockSpec((1, tn), lambda i, j: (0, j)),
            pl.BlockSpec((1, tn), lambda i, j: (0, j + nj)),
        ],
        out_specs=pl.BlockSpec((tm, tn), lambda i, j: (i, j)),
        out_shape=jax.ShapeDtypeStruct((m, d), F32),
        scratch_shapes=[pltpu.VMEM((tm, d), BF16)],
        compiler_params=pltpu.CompilerParams(
            dimension_semantics=("parallel", "arbitrary"),
            vmem_limit_bytes=VMEM_LIMIT),
        name="pw1_glu",
    )(x, g, w, w, b, b)


def _dwconv_ln_kernel(*refs, tc, tiles_per_seq, n_casts):
    u_ref, halo_ref, w_ref, b_ref, lg_ref, lb_ref = refs[:6]
    cast_in = refs[6:6 + n_casts]
    o_ref = refs[6 + n_casts]
    cast_out = refs[7 + n_casts:7 + 2 * n_casts]
    sh_ref, conv_ref = refs[-2:]
    _emit_casts(cast_in, cast_out)
    d = u_ref.shape[1]
    first = (pl.program_id(0) % tiles_per_seq) == 0
    halo = halo_ref[...]
    sh_ref[0, 0:HALO, :] = jnp.where(first, jnp.zeros_like(halo), halo)
    sh_ref[0, HALO:, :] = u_ref[...]
    n = tc + HALO - SUBLANES
    for s in range(1, SUBLANES):
        sh_ref[s, 0:n, :] = sh_ref[0, s:s + n, :]

    off = HALO - (CONV_WIDTH - 1)

    def row_body(ri, carry):
        r = pl.multiple_of(ri * CONV_ROWS, CONV_ROWS)
        groups = range(0, CONV_ROWS, SUBLANES)
        for c in range(0, d, CONV_COLS):
            cs = slice(c, c + CONV_COLS)
            bias = b_ref[:, cs]
            accs = [bias for _ in groups]
            for k in range(CONV_WIDTH):
                a, s = divmod(k + off, SUBLANES)
                wk = w_ref[k, :, cs]
                accs = [acc + sh_ref[s, pl.ds(r + SUBLANES * a + g, SUBLANES), cs] * wk
                        for acc, g in zip(accs, groups)]
            for acc, g in zip(accs, groups):
                conv_ref[pl.ds(r + g, SUBLANES), cs] = acc
        return carry

    lax.fori_loop(0, tc // CONV_ROWS, row_body, 0)

    y = conv_ref[...]
    mu = jnp.mean(y, axis=-1, keepdims=True)
    yc = y - mu
    var = jnp.mean(yc * yc, axis=-1, keepdims=True)
    z = yc * lax.rsqrt(var + LN_EPS) * lg_ref[...] + lb_ref[...]
    o_ref[...] = (z * jax.nn.sigmoid(z)).astype(o_ref.dtype)


def _dwconv_ln(u, w_dw, b_dw, ln_g, ln_b, casts=(), *, seq, tc=256):
    m, d = u.shape
    halo_blocks = tc // HALO
    grid = (m // tc,)
    c_in, c_out, c_shapes, c_args = _cast_specs(casts, grid[0], lambda i: i)
    w_rep = jnp.broadcast_to(w_dw[:, None, :], (CONV_WIDTH, SUBLANES, d))
    b_rep = jnp.broadcast_to(b_dw, (SUBLANES, d))
    outs = pl.pallas_call(
        functools.partial(_dwconv_ln_kernel, tc=tc, tiles_per_seq=seq // tc, n_casts=len(casts)),
        grid=grid,
        in_specs=[
            pl.BlockSpec((tc, d), lambda i: (i, 0)),
            pl.BlockSpec((HALO, d), lambda i: (jnp.maximum(i * halo_blocks - 1, 0), 0)),
            pl.BlockSpec((CONV_WIDTH, SUBLANES, d), lambda i: (0, 0, 0)),
            pl.BlockSpec((SUBLANES, d), lambda i: (0, 0)),
            pl.BlockSpec((1, d), lambda i: (0, 0)),
            pl.BlockSpec((1, d), lambda i: (0, 0)),
        ] + c_in,
        out_specs=[pl.BlockSpec((tc, d), lambda i: (i, 0))] + c_out,
        out_shape=[jax.ShapeDtypeStruct((m, d), BF16)] + c_shapes,
        scratch_shapes=[pltpu.VMEM((SUBLANES, tc + HALO, d), F32), pltpu.VMEM((tc, d), F32)],
        compiler_params=pltpu.CompilerParams(
            dimension_semantics=("parallel",),
            vmem_limit_bytes=VMEM_LIMIT),
        name="dwconv_ln",
    )(u, u, w_rep, b_rep, ln_g, ln_b, *c_args)
    return outs[0], tuple(outs[1:])


def _linear_residual_kernel(*refs, has_bias):
    if has_bias:
        x_ref, a_ref, w_ref, b_ref, o_ref = refs
    else:
        x_ref, a_ref, w_ref, o_ref = refs
    y = jnp.dot(a_ref[...], w_ref[...], preferred_element_type=F32)
    if has_bias:
        y = y + b_ref[...]
    o_ref[...] = x_ref[...] + y


def _linear_residual(x, a, w, b=None, *, tm=512, tn=D_MODEL):
    m, n = x.shape
    k = a.shape[1]
    in_specs = [
        pl.BlockSpec((tm, tn), lambda i, j: (i, j)),
        pl.BlockSpec((tm, k), lambda i, j: (i, 0)),
        pl.BlockSpec((k, tn), lambda i, j: (0, j)),
    ]
    args = [x, a, w]
    if b is not None:
        in_specs.append(pl.BlockSpec((1, tn), lambda i, j: (0, j)))
        args.append(b)
    return pl.pallas_call(
        functools.partial(_linear_residual_kernel, has_bias=b is not None),
        grid=(m // tm, n // tn),
        in_specs=in_specs,
        out_specs=pl.BlockSpec((tm, tn), lambda i, j: (i, j)),
        out_shape=jax.ShapeDtypeStruct((m, n), F32),
        compiler_params=pltpu.CompilerParams(
            dimension_semantics=("parallel", "arbitrary"),
            vmem_limit_bytes=VMEM_LIMIT),
        name="linear_residual",
    )(*args)


def _mlp_kernel(*refs, tm, final_norm):
    if final_norm:
        x_ref, g_ref, w1_ref, w2_ref, fg_ref, o_ref, xn_ref = refs
    else:
        x_ref, g_ref, w1_ref, w2_ref, o_ref, xn_ref = refs
    f = pl.program_id(1)

    @pl.when(f == 0)
    def _():
        _rmsnorm_into(x_ref, g_ref, xn_ref, tm)
        o_ref[...] = x_ref[...]

    h = jnp.dot(xn_ref[...], w1_ref[...], preferred_element_type=F32)
    h = jnp.square(jnp.maximum(h, 0.0)).astype(BF16)
    o_ref[...] += jnp.dot(h, w2_ref[...], preferred_element_type=F32)

    if final_norm:
        @pl.when(f == pl.num_programs(1) - 1)
        def _():
            _rmsnorm_into(o_ref, fg_ref, o_ref, tm)


def _mlp(x, g, w1, w2, final_g=None, *, tm=1024, tf=1024):
    m, d = x.shape
    dff = w1.shape[1]
    final_norm = final_g is not None
    in_specs = [
        pl.BlockSpec((tm, d), lambda i, f: (i, 0)),
        pl.BlockSpec((1, d), lambda i, f: (0, 0)),
        pl.BlockSpec((d, tf), lambda i, f: (0, f)),
        pl.BlockSpec((tf, d), lambda i, f: (f, 0)),
    ]
    args = [x, g, w1, w2]
    if final_norm:
        in_specs.append(pl.BlockSpec((1, d), lambda i, f: (0, 0)))
        args.append(final_g)
    return pl.pallas_call(
        functools.partial(_mlp_kernel, tm=tm, final_norm=final_norm),
        grid=(m // tm, dff // tf),
        in_specs=in_specs,
        out_specs=pl.BlockSpec((tm, d), lambda i, f: (i, 0)),
        out_shape=jax.ShapeDtypeStruct((m, d), F32),
        scratch_shapes=[pltpu.VMEM((tm, d), BF16)],
        compiler_params=pltpu.CompilerParams(
            dimension_semantics=("parallel", "arbitrary"),
            vmem_limit_bytes=BIG_VMEM_LIMIT),
        name="mlp",
    )(*args)


def _rope_table_kernel(pos_ref, invf_ref, cos_t_ref, sin_t_ref, cos_r_ref, sin_r_ref):
    ang = pos_ref[...].astype(F32) * invf_ref[...]
    c = jnp.cos(ang)
    s = jnp.sin(ang)
    cos_t_ref[...] = c
    sin_t_ref[...] = s
    z = jnp.zeros((LANES - QK_ROPE_DIM, ang.shape[1]), F32)
    cos_r_ref[...] = jnp.concatenate([c, c, z], axis=0).T
    sin_r_ref[...] = jnp.concatenate([s, s, z], axis=0).T


def _rope_tables(positions, *, tr=2048):
    half = QK_ROPE_DIM // 2
    m = positions.size
    inv_freq = ROPE_THETA ** (-jnp.arange(0, QK_ROPE_DIM, 2, dtype=F32) / QK_ROPE_DIM)
    return pl.pallas_call(
        _rope_table_kernel,
        grid=(m // tr,),
        in_specs=[pl.BlockSpec((1, tr), lambda i: (0, i)),
                  pl.BlockSpec((half, 1), lambda i: (0, 0))],
        out_specs=[pl.BlockSpec((half, tr), lambda i: (0, i)),
                   pl.BlockSpec((half, tr), lambda i: (0, i)),
                   pl.BlockSpec((tr, LANES), lambda i: (i, 0)),
                   pl.BlockSpec((tr, LANES), lambda i: (i, 0))],
        out_shape=[jax.ShapeDtypeStruct((half, m), F32)] * 2
                  + [jax.ShapeDtypeStruct((m, LANES), F32)] * 2,
        compiler_params=pltpu.CompilerParams(dimension_semantics=("parallel",)),
        name="rope_tables",
    )(positions.reshape(1, m), inv_freq.reshape(half, 1))


def _mla_proj_kernel(x_ref, g_ref, cos_t_ref, sin_t_ref, cos_r_ref, sin_r_ref, win_ref, qg_ref, kvg_ref,
                     wkpe_ref, wq_ref, wqs_ref, wkv_ref, q_ref, k_ref, v_ref, wkn_ref, wv_ref, *, q_scale):
    @pl.when(pl.program_id(0) == 0)
    def _():
        for h in range(N_HEADS):
            src = h * (QK_NOPE_DIM + V_HEAD_DIM)
            wkn_ref[:, h * QK_NOPE_DIM:(h + 1) * QK_NOPE_DIM] = wkv_ref[:, src:src + QK_NOPE_DIM]
            wv_ref[:, h * V_HEAD_DIM:(h + 1) * V_HEAD_DIM] = (
                wkv_ref[:, src + QK_NOPE_DIM:src + QK_NOPE_DIM + V_HEAD_DIM])

    xn = _rms_rows(x_ref[...], g_ref[...], NORM_EPS).astype(BF16)
    base = Q_LORA_RANK + KV_LORA_RANK
    down = jnp.dot(xn, win_ref[:, :base], preferred_element_type=F32)
    c_q = down[:, :Q_LORA_RANK]
    c_kv = down[:, Q_LORA_RANK:]
    kp = jnp.dot(xn, wkpe_ref[...], preferred_element_type=F32)
    kpe = (kp[:, :LANES] * cos_r_ref[...] + kp[:, LANES:] * sin_r_ref[...]).astype(k_ref.dtype)

    cqn = _rms_rows(c_q, qg_ref[...], NORM_EPS).astype(BF16)
    ckvn = _rms_rows(c_kv, kvg_ref[...], NORM_EPS).astype(BF16)

    q = lax.dot_general(wq_ref[...], cqn, _TT, preferred_element_type=F32)
    qs = lax.dot_general(wqs_ref[...], cqn, _TT, preferred_element_type=F32)
    cos2 = jnp.concatenate([cos_t_ref[...]] * 2, axis=0)
    sin2 = jnp.concatenate([sin_t_ref[...]] * 2, axis=0)
    zpad = jnp.zeros((QK_PAD_DIM - QK_HEAD_DIM, q.shape[1]), q_ref.dtype)
    for h in range(N_HEADS):
        src, dst = h * QK_HEAD_DIM, h * QK_PAD_DIM
        q_ref[dst:dst + QK_NOPE_DIM, :] = (q[src:src + QK_NOPE_DIM] * q_scale).astype(q_ref.dtype)
        rot = (q[src + QK_NOPE_DIM:src + QK_HEAD_DIM] * cos2
               + qs[h * QK_ROPE_DIM:(h + 1) * QK_ROPE_DIM] * sin2)
        q_ref[dst + QK_NOPE_DIM:dst + QK_HEAD_DIM, :] = (rot * q_scale).astype(q_ref.dtype)
        q_ref[dst + QK_HEAD_DIM:dst + QK_PAD_DIM, :] = zpad

    kn = jnp.dot(ckvn, wkn_ref[...], preferred_element_type=F32).astype(k_ref.dtype)
    for h in range(N_HEADS):
        dst = h * QK_PAD_DIM
        k_ref[:, dst:dst + QK_NOPE_DIM] = kn[:, h * QK_NOPE_DIM:(h + 1) * QK_NOPE_DIM]
        k_ref[:, dst + QK_NOPE_DIM:dst + QK_PAD_DIM] = kpe
    v = lax.dot_general(wv_ref[...], ckvn, _TT, preferred_element_type=F32).astype(v_ref.dtype)
    ones = jnp.ones((V_PAD_DIM - V_HEAD_DIM, v.shape[1]), v_ref.dtype)
    for h in range(N_HEADS):
        v_ref[h * V_PAD_DIM:h * V_PAD_DIM + V_HEAD_DIM, :] = v[h * V_HEAD_DIM:(h + 1) * V_HEAD_DIM]
        v_ref[h * V_PAD_DIM + V_HEAD_DIM:(h + 1) * V_PAD_DIM, :] = ones


def _mla_proj(x, g, tables, weights, layer, q_g, kv_g, *, q_scale, tm=512):
    m, d = x.shape
    cos_t, sin_t, cos_r, sin_r = tables

    def resident(a):
        return pl.BlockSpec(a.shape, lambda i: (0,) * a.ndim)

    def layer_resident(a):
        return pl.BlockSpec((None,) + a.shape[1:], lambda i: (layer,) + (0,) * (a.ndim - 1))

    def rows(width):
        return pl.BlockSpec((tm, width), lambda i: (i, 0))

    def cols(height):
        return pl.BlockSpec((height, tm), lambda i: (0, i))

    half = QK_ROPE_DIM // 2
    return pl.pallas_call(
        functools.partial(_mla_proj_kernel, q_scale=q_scale),
        grid=(m // tm,),
        in_specs=[rows(d), resident(g), cols(half), cols(half), rows(LANES), rows(LANES),
                  layer_resident(weights[0]), resident(q_g), resident(kv_g)]
                 + [layer_resident(w) for w in weights[1:]],
        out_specs=[cols(N_HEADS * QK_PAD_DIM), rows(N_HEADS * QK_PAD_DIM),
                   pl.BlockSpec((None, N_HEADS * V_PAD_DIM, tm), lambda i: (i, 0, 0))],
        out_shape=[jax.ShapeDtypeStruct((N_HEADS * QK_PAD_DIM, m), BF16),
                   jax.ShapeDtypeStruct((m, N_HEADS * QK_PAD_DIM), BF16),
                   jax.ShapeDtypeStruct((m // tm, N_HEADS * V_PAD_DIM, tm), BF16)],
        scratch_shapes=[pltpu.VMEM((KV_LORA_RANK, N_HEADS * QK_NOPE_DIM), BF16),
                        pltpu.VMEM((KV_LORA_RANK, N_HEADS * V_HEAD_DIM), BF16)],
        compiler_params=pltpu.CompilerParams(
            dimension_semantics=("arbitrary",),
            vmem_limit_bytes=BIG_VMEM_LIMIT),
        name="mla_proj",
    )(x, g, cos_t, sin_t, cos_r, sin_r, weights[0], q_g, kv_g, *weights[1:])


def _attn_kernel(*refs, tq, tk, tv, hb, n_casts):
    q_ref, qn_ref, k_ref, v_ref = refs[:4]
    cast_in = refs[4:4 + n_casts]
    o_ref = refs[4 + n_casts]
    cast_out = refs[5 + n_casts:5 + 2 * n_casts]
    s_scr, m_scr, acc_scr = refs[-3:]
    _emit_casts(cast_in, cast_out)
    qi = pl.program_id(2)
    nv = tk // tv
    m_scr[...] = jnp.full(m_scr.shape, -jnp.inf, F32)
    acc_scr[...] = jnp.zeros(acc_scr.shape, F32)

    def scores(kb, c):
        r = pl.multiple_of(kb * tk, tk)
        return jnp.dot(k_ref[pl.ds(r, tk), c * QK_PAD_DIM:(c + 1) * QK_PAD_DIM],
                       q_ref[c * QK_PAD_DIM:(c + 1) * QK_PAD_DIM, :], preferred_element_type=F32)

    def update(kb, c, s):
        m_prev = m_scr[c]
        m_new = jnp.maximum(m_prev, jnp.max(s, axis=0, keepdims=True))
        alpha = jnp.exp2(m_prev - m_new)
        p = jnp.exp2(s - m_new).astype(BF16)
        m_scr[c] = m_new
        vs = slice(c * V_PAD_DIM, (c + 1) * V_PAD_DIM)
        pv = jnp.dot(v_ref[kb * nv, vs, :], p[:tv], preferred_element_type=F32)
        for t in range(1, nv):
            pv = pv + jnp.dot(v_ref[kb * nv + t, vs, :], p[t * tv:(t + 1) * tv],
                              preferred_element_type=F32)
        acc_scr[c] = alpha * acc_scr[c] + pv

    def advance(kb, cur, nxt):
        for c in range(hb):
            s_scr[nxt, c] = scores(kb + 1, c)
            update(kb, c, s_scr[cur, c])

    @pl.when(qi == 0)
    def _():
        for c in range(hb):
            s_scr[0, c] = scores(0, c)

    def body(t, carry):
        advance(2 * t, 0, 1)
        advance(2 * t + 1, 1, 0)
        return carry

    lax.fori_loop(0, qi // 2, body, 0)

    @pl.when(qi % 2 == 1)
    def _():
        advance(qi - 1, 0, 1)

    k_chunk = lax.broadcasted_iota(jnp.int32, (tk, tq), 0) // CHUNK
    q_chunk = lax.broadcasted_iota(jnp.int32, (tk, tq), 1) // CHUNK
    visible = k_chunk <= q_chunk

    def finish(slot):
        for c in range(hb):
            update(qi, c, jnp.where(visible, s_scr[slot, c], -jnp.inf))
            s_scr[0, c] = jnp.dot(k_ref[0:tk, c * QK_PAD_DIM:(c + 1) * QK_PAD_DIM],
                                  qn_ref[c * QK_PAD_DIM:(c + 1) * QK_PAD_DIM, :],
                                  preferred_element_type=F32)
            acc = acc_scr[c]
            o_ref[:, c * V_HEAD_DIM:(c + 1) * V_HEAD_DIM] = (
                acc[:V_HEAD_DIM] / acc[V_HEAD_DIM:V_HEAD_DIM + 1]).T.astype(o_ref.dtype)

    for parity in range(2):
        pl.when(qi % 2 == parity)(functools.partial(finish, parity))


def _attention(q_t, k, v_t, casts=(), *, batch, seq, tq=512, hb=4):
    tk = tq
    nq = seq // tq
    ng = N_HEADS // hb
    tv = v_t.shape[2]
    grid = (batch, ng, nq)
    c_in, c_out, c_shapes, c_args = _cast_specs(
        casts, batch * ng * nq, lambda b, h, i: (b * ng + h) * nq + i)
    outs = pl.pallas_call(
        functools.partial(_attn_kernel, tq=tq, tk=tk, tv=tv, hb=hb, n_casts=len(casts)),
        grid=grid,
        in_specs=[
            pl.BlockSpec((hb * QK_PAD_DIM, tq), lambda b, h, i: (h, b * nq + i)),
            pl.BlockSpec((hb * QK_PAD_DIM, tq), lambda b, h, i: (h, b * nq + jnp.minimum(i + 1, nq - 1))),
            pl.BlockSpec((seq, hb * QK_PAD_DIM), lambda b, h, i: (b, h)),
            pl.BlockSpec((seq // tv, hb * V_PAD_DIM, tv), lambda b, h, i: (b, h, 0)),
        ] + c_in,
        out_specs=[pl.BlockSpec((tq, hb * V_HEAD_DIM), lambda b, h, i: (b * nq + i, h))] + c_out,
        out_shape=[jax.ShapeDtypeStruct((batch * seq, N_HEADS * V_HEAD_DIM), BF16)] + c_shapes,
        scratch_shapes=[pltpu.VMEM((2, hb, tk, tq), F32), pltpu.VMEM((hb, 1, tq), F32),
                        pltpu.VMEM((hb, V_PAD_DIM, tq), F32)],
        compiler_params=pltpu.CompilerParams(
            dimension_semantics=("parallel", "parallel", "arbitrary"),
            vmem_limit_bytes=VMEM_LIMIT),
        name="mla_attention",
    )(q_t, q_t, k, v_t, *c_args)
    return outs[0], tuple(outs[1:])


def _prep_mla_weights(w_in, w_q_up, w_kv_up):
    w_in, w_q_up, w_kv_up = (w.astype(BF16) for w in (w_in, w_q_up, w_kv_up))
    nl, d, _ = w_in.shape
    half = QK_ROPE_DIM // 2
    base = Q_LORA_RANK + KV_LORA_RANK
    pad = jnp.zeros((nl, d, LANES - QK_ROPE_DIM), w_in.dtype)
    k_pe = w_in[:, :, base:]
    w_kpe = jnp.concatenate([k_pe, pad, -k_pe[:, :, half:], k_pe[:, :, :half], pad], axis=2)

    r = w_q_up.shape[1]
    pe = w_q_up.reshape(nl, r, N_HEADS, QK_HEAD_DIM)[:, :, :, QK_NOPE_DIM:]
    wq_swap = jnp.concatenate([-pe[..., half:], pe[..., :half]], axis=3).reshape(nl, r, -1)

    return w_in, w_kpe, w_q_up, wq_swap, w_kv_up


def kernel(x, positions, norm_mixer_g, norm_mlp_g, conv_w_pw1, conv_b_pw1, conv_w_dw, conv_b_dw, conv_ln_g, conv_ln_b, conv_w_pw2, conv_b_pw2, mla_w_in, mla_q_norm_g, mla_kv_norm_g, mla_w_q_up, mla_w_kv_up, mla_w_o, mlp_w1, mlp_w2, final_norm_g):
    batch, seq, d = x.shape
    m = batch * seq
    xs = x.reshape(m, d)

    def row(v):
        return v.reshape(1, -1)

    mla_weights = _prep_mla_weights(mla_w_in, mla_w_q_up, mla_w_kv_up)
    tables = _rope_tables(positions)
    q_scale = QK_HEAD_DIM ** -0.5 * math.log2(math.e)

    w_pw1 = conv_w_pw1[0].astype(BF16)
    w_o = None
    for layer in range(DEPTH):
        j = layer // N_MIXERS
        g_mix = row(norm_mixer_g[layer])
        if layer % N_MIXERS == 0:
            u = _pw1_glu(xs, g_mix, w_pw1, row(conv_b_pw1[j]))
            a, (w1, w2, w_pw2, w_o) = _dwconv_ln(
                u, conv_w_dw[j], row(conv_b_dw[j]), row(conv_ln_g[j]), row(conv_ln_b[j]),
                ((mlp_w1, layer), (mlp_w2, layer), (conv_w_pw2, j), (mla_w_o, j)), seq=seq)
            xs = _linear_residual(xs, a, w_pw2, row(conv_b_pw2[j]))
        else:
            q, k, v = _mla_proj(xs, g_mix, tables, mla_weights, j, row(mla_q_norm_g[j]),
                                row(mla_kv_norm_g[j]), q_scale=q_scale)
            casts = [(mlp_w1, layer), (mlp_w2, layer)]
            if j + 1 < conv_w_pw1.shape[0]:
                casts.append((conv_w_pw1, j + 1))
            o, (w1, w2, *rest) = _attention(q, k, v, tuple(casts), batch=batch, seq=seq)
            w_pw1 = rest[0] if rest else None
            xs = _linear_residual(xs, o, w_o)
        final_g = row(final_norm_g) if layer == DEPTH - 1 else None
        xs = _mlp(xs, row(norm_mlp_g[layer]), w1, w2, final_g)
    return xs.reshape(batch, seq, d)
```

```python
import functools
import math

import jax
import jax.numpy as jnp
from jax import lax
from jax.experimental import pallas as pl
from jax.experimental.pallas import tpu as pltpu

D_MODEL = 2048
DEPTH = 4
CHUNK = 64
N_MIXERS = 2
CONV_WIDTH = 31
N_HEADS = 16
QK_NOPE_DIM = 128
QK_ROPE_DIM = 64
V_HEAD_DIM = 128
Q_LORA_RANK = 512
KV_LORA_RANK = 512
D_FF = 4 * D_MODEL
ROPE_THETA = 10000.0
NORM_EPS = 1e-6
LN_EPS = 1e-5

SUBLANES = 8
LANES = 128
QK_HEAD_DIM = QK_NOPE_DIM + QK_ROPE_DIM
V_PAD_DIM = V_HEAD_DIM + 16
QK_PAD_DIM = 2 * LANES
HALO = 32
CONV_ROWS = 64
CONV_COLS = 512
VMEM_LIMIT = 56 * 1024 * 1024
BIG_VMEM_LIMIT = 60 * 1024 * 1024

F32 = jnp.float32
BF16 = jnp.bfloat16

_TT = (((0,), (1,)), ((), ()))


def _rms_rows(x, g, eps):
    ms = jnp.mean(x * x, axis=-1, keepdims=True)
    return x * lax.rsqrt(ms + eps) * g


def _rmsnorm_into(x_ref, g_ref, out_ref, rows, chunk=256):
    g = g_ref[...]

    def body(c, carry):
        r = pl.multiple_of(c * chunk, chunk)
        x = x_ref[pl.ds(r, chunk), :]
        out_ref[pl.ds(r, chunk), :] = _rms_rows(x, g, NORM_EPS).astype(out_ref.dtype)
        return carry

    lax.fori_loop(0, rows // chunk, body, 0)


def _cast_specs(casts, n_steps, step_of):
    in_specs, out_specs, out_shapes, args = [], [], [], []
    for w_all, layer in casts:
        _, r, c = w_all.shape
        blk = (r // n_steps, c)
        in_specs.append(pl.BlockSpec((None,) + blk, lambda *ids, layer=layer: (layer, step_of(*ids), 0)))
        out_specs.append(pl.BlockSpec(blk, lambda *ids: (step_of(*ids), 0)))
        out_shapes.append(jax.ShapeDtypeStruct((r, c), BF16))
        args.append(w_all)
    return in_specs, out_specs, out_shapes, args


def _emit_casts(cast_in, cast_out):
    for src, dst in zip(cast_in, cast_out):
        dst[...] = src[...].astype(dst.dtype)


def _pw1_glu_kernel(x_ref, g_ref, wa_ref, wg_ref, ba_ref, bg_ref, o_ref, xn_ref, *, tm):
    @pl.when(pl.program_id(1) == 0)
    def _():
        _rmsnorm_into(x_ref, g_ref, xn_ref, tm)

    xn = xn_ref[...]
    tn = o_ref.shape[1]
    for cs in (slice(0, tn // 2), slice(tn // 2, tn)):
        a = jnp.dot(xn, wa_ref[:, cs], preferred_element_type=F32) + ba_ref[:, cs]
        gate = jnp.dot(xn, wg_ref[:, cs], preferred_element_type=F32) + bg_ref[:, cs]
        o_ref[:, cs] = a * jax.nn.sigmoid(gate)


def _pw1_glu(x, g, w, b, *, tm=1024, tn=1024):
    m, d = x.shape
    nj = d // tn
    return pl.pallas_call(
        functools.partial(_pw1_glu_kernel, tm=tm),
        grid=(m // tm, nj),
        in_specs=[
            pl.BlockSpec((tm, d), lambda i, j: (i, 0)),
            pl.BlockSpec((1, d), lambda i, j: (0, 0)),
            pl.BlockSpec((d, tn), lambda i, j: (0, j)),
            pl.BlockSpec((d, tn), lambda i, j: (0, j + nj)),
            pl.BlockSpec((1, tn), lambda i, j: (0, j)),
            pl.BlockSpec((1, tn), lambda i, j: (0, j + nj)),
        ],
        out_specs=pl.BlockSpec((tm, tn), lambda i, j: (i, j)),
        out_shape=jax.ShapeDtypeStruct((m, d), F32),
        scratch_shapes=[pltpu.VMEM((tm, d), BF16)],
        compiler_params=pltpu.CompilerParams(
            dimension_semantics=("parallel", "arbitrary"),
            vmem_limit_bytes=VMEM_LIMIT),
        name="pw1_glu",
    )(x, g, w, w, b, b)


def _dwconv_ln_kernel(*refs, tc, tiles_per_seq, n_casts):
    u_ref, halo_ref, w_ref, b_ref, lg_ref, lb_ref = refs[:6]
    cast_in = refs[6:6 + n_casts]
    o_ref = refs[6 + n_casts]
    cast_out = refs[7 + n_casts:7 + 2 * n_casts]
    sh_ref, conv_ref = refs[-2:]
    _emit_casts(cast_in, cast_out)
    d = u_ref.shape[1]
    first = (pl.program_id(0) % tiles_per_seq) == 0
    halo = halo_ref[...]
    sh_ref[0, 0:HALO, :] = jnp.where(first, jnp.zeros_like(halo), halo)
    sh_ref[0, HALO:, :] = u_ref[...]
    n = tc + HALO - SUBLANES
    for s in range(1, SUBLANES):
        sh_ref[s, 0:n, :] = sh_ref[0, s:s + n, :]

    off = HALO - (CONV_WIDTH - 1)

    def row_body(ri, carry):
        r = pl.multiple_of(ri * CONV_ROWS, CONV_ROWS)
        groups = range(0, CONV_ROWS, SUBLANES)
        for c in range(0, d, CONV_COLS):
            cs = slice(c, c + CONV_COLS)
            bias = b_ref[:, cs]
            accs = [bias for _ in groups]
            for k in range(CONV_WIDTH):
                a, s = divmod(k + off, SUBLANES)
                wk = w_ref[k, :, cs]
                accs = [acc + sh_ref[s, pl.ds(r + SUBLANES * a + g, SUBLANES), cs] * wk
                        for acc, g in zip(accs, groups)]
            for acc, g in zip(accs, groups):
                conv_ref[pl.ds(r + g, SUBLANES), cs] = acc
        return carry

    lax.fori_loop(0, tc // CONV_ROWS, row_body, 0)

    y = conv_ref[...]
    mu = jnp.mean(y, axis=-1, keepdims=True)
    yc = y - mu
    var = jnp.mean(yc * yc, axis=-1, keepdims=True)
    z = yc * lax.rsqrt(var + LN_EPS) * lg_ref[...] + lb_ref[...]
    o_ref[...] = (z * jax.nn.sigmoid(z)).astype(o_ref.dtype)


def _dwconv_ln(u, w_dw, b_dw, ln_g, ln_b, casts=(), *, seq, tc=256):
    m, d = u.shape
    halo_blocks = tc // HALO
    grid = (m // tc,)
    c_in, c_out, c_shapes, c_args = _cast_specs(casts, grid[0], lambda i: i)
    w_rep = jnp.broadcast_to(w_dw[:, None, :], (CONV_WIDTH, SUBLANES, d))
    b_rep = jnp.broadcast_to(b_dw, (SUBLANES, d))
    outs = pl.pallas_call(
        functools.partial(_dwconv_ln_kernel, tc=tc, tiles_per_seq=seq // tc, n_casts=len(casts)),
        grid=grid,
        in_specs=[
            pl.BlockSpec((tc, d), lambda i: (i, 0)),
            pl.BlockSpec((HALO, d), lambda i: (jnp.maximum(i * halo_blocks - 1, 0), 0)),
            pl.BlockSpec((CONV_WIDTH, SUBLANES, d), lambda i: (0, 0, 0)),
            pl.BlockSpec((SUBLANES, d), lambda i: (0, 0)),
            pl.BlockSpec((1, d), lambda i: (0, 0)),
            pl.BlockSpec((1, d), lambda i: (0, 0)),
        ] + c_in,
        out_specs=[pl.BlockSpec((tc, d), lambda i: (i, 0))] + c_out,
        out_shape=[jax.ShapeDtypeStruct((m, d), BF16)] + c_shapes,
        scratch_shapes=[pltpu.VMEM((SUBLANES, tc + HALO, d), F32), pltpu.VMEM((tc, d), F32)],
        compiler_params=pltpu.CompilerParams(
            dimension_semantics=("parallel",),
            vmem_limit_bytes=VMEM_LIMIT),
        name="dwconv_ln",
    )(u, u, w_rep, b_rep, ln_g, ln_b, *c_args)
    return outs[0], tuple(outs[1:])


def _linear_residual_kernel(*refs, has_bias):
    if has_bias:
        x_ref, a_ref, w_ref, b_ref, o_ref = refs
    else:
        x_ref, a_ref, w_ref, o_ref = refs
    y = jnp.dot(a_ref[...], w_ref[...], preferred_element_type=F32)
    if has_bias:
        y = y + b_ref[...]
    o_ref[...] = x_ref[...] + y


def _linear_residual(x, a, w, b=None, *, tm=512, tn=D_MODEL):
    m, n = x.shape
    k = a.shape[1]
    in_specs = [
        pl.BlockSpec((tm, tn), lambda i, j: (i, j)),
        pl.BlockSpec((tm, k), lambda i, j: (i, 0)),
        pl.BlockSpec((k, tn), lambda i, j: (0, j)),
    ]
    args = [x, a, w]
    if b is not None:
        in_specs.append(pl.BlockSpec((1, tn), lambda i, j: (0, j)))
        args.append(b)
    return pl.pallas_call(
        functools.partial(_linear_residual_kernel, has_bias=b is not None),
        grid=(m // tm, n // tn),
        in_specs=in_specs,
        out_specs=pl.BlockSpec((tm, tn), lambda i, j: (i, j)),
        out_shape=jax.ShapeDtypeStruct((m, n), F32),
        compiler_params=pltpu.CompilerParams(
            dimension_semantics=("parallel", "arbitrary"),
            vmem_limit_bytes=VMEM_LIMIT),
        name="linear_residual",
    )(*args)


def _mlp_kernel(*refs, tm, final_norm):
    if final_norm:
        x_ref, g_ref, w1_ref, w2_ref, fg_ref, o_ref, xn_ref = refs
    else:
        x_ref, g_ref, w1_ref, w2_ref, o_ref, xn_ref = refs
    f = pl.program_id(1)

    @pl.when(f == 0)
    def _():
        _rmsnorm_into(x_ref, g_ref, xn_ref, tm)
        o_ref[...] = x_ref[...]

    h = jnp.dot(xn_ref[...], w1_ref[...], preferred_element_type=F32)
    h = jnp.square(jnp.maximum(h, 0.0)).astype(BF16)
    o_ref[...] += jnp.dot(h, w2_ref[...], preferred_element_type=F32)

    if final_norm:
        @pl.when(f == pl.num_programs(1) - 1)
        def _():
            _rmsnorm_into(o_ref, fg_ref, o_ref, tm)


def _mlp(x, g, w1, w2, final_g=None, *, tm=1024, tf=1024):
    m, d = x.shape
    dff = w1.shape[1]
    final_norm = final_g is not None
    in_specs = [
        pl.BlockSpec((tm, d), lambda i, f: (i, 0)),
        pl.BlockSpec((1, d), lambda i, f: (0, 0)),
        pl.BlockSpec((d, tf), lambda i, f: (0, f)),
        pl.BlockSpec((tf, d), lambda i, f: (f, 0)),
    ]
    args = [x, g, w1, w2]
    if final_norm:
        in_specs.append(pl.BlockSpec((1, d), lambda i, f: (0, 0)))
        args.append(final_g)
    return pl.pallas_call(
        functools.partial(_mlp_kernel, tm=tm, final_norm=final_norm),
        grid=(m // tm, dff // tf),
        in_specs=in_specs,
        out_specs=pl.BlockSpec((tm, d), lambda i, f: (i, 0)),
        out_shape=jax.ShapeDtypeStruct((m, d), F32),
        scratch_shapes=[pltpu.VMEM((tm, d), BF16)],
        compiler_params=pltpu.CompilerParams(
            dimension_semantics=("parallel", "arbitrary"),
            vmem_limit_bytes=BIG_VMEM_LIMIT),
        name="mlp",
    )(*args)


def _rope_table_kernel(pos_ref, invf_ref, cos_t_ref, sin_t_ref, cos_r_ref, sin_r_ref):
    ang = pos_ref[...].astype(F32) * invf_ref[...]
    c = jnp.cos(ang)
    s = jnp.sin(ang)
    cos_t_ref[...] = c
    sin_t_ref[...] = s
    z = jnp.zeros((LANES - QK_ROPE_DIM, ang.shape[1]), F32)
    cos_r_ref[...] = jnp.concatenate([c, c, z], axis=0).T
    sin_r_ref[...] = jnp.concatenate([s, s, z], axis=0).T


def _rope_tables(positions, *, tr=2048):
    half = QK_ROPE_DIM // 2
    m = positions.size
    inv_freq = ROPE_THETA ** (-jnp.arange(0, QK_ROPE_DIM, 2, dtype=F32) / QK_ROPE_DIM)
    return pl.pallas_call(
        _rope_table_kernel,
        grid=(m // tr,),
        in_specs=[pl.BlockSpec((1, tr), lambda i: (0, i)),
                  pl.BlockSpec((half, 1), lambda i: (0, 0))],
        out_specs=[pl.BlockSpec((half, tr), lambda i: (0, i)),
                   pl.BlockSpec((half, tr), lambda i: (0, i)),
                   pl.BlockSpec((tr, LANES), lambda i: (i, 0)),
                   pl.BlockSpec((tr, LANES), lambda i: (i, 0))],
        out_shape=[jax.ShapeDtypeStruct((half, m), F32)] * 2
                  + [jax.ShapeDtypeStruct((m, LANES), F32)] * 2,
        compiler_params=pltpu.CompilerParams(dimension_semantics=("parallel",)),
        name="rope_tables",
    )(positions.reshape(1, m), inv_freq.reshape(half, 1))


def _mla_proj_kernel(x_ref, g_ref, cos_t_ref, sin_t_ref, cos_r_ref, sin_r_ref, win_ref, qg_ref, kvg_ref,
                     wkpe_ref, wq_ref, wkv_ref, q_ref, k_ref, v_ref, wkn_ref, wv_ref, *, q_scale):
    @pl.when(pl.program_id(0) == 0)
    def _():
        for h in range(N_HEADS):
            src = h * (QK_NOPE_DIM + V_HEAD_DIM)
            wkn_ref[:, h * QK_NOPE_DIM:(h + 1) * QK_NOPE_DIM] = wkv_ref[:, src:src + QK_NOPE_DIM]
            wv_ref[:, h * V_HEAD_DIM:(h + 1) * V_HEAD_DIM] = (
                wkv_ref[:, src + QK_NOPE_DIM:src + QK_NOPE_DIM + V_HEAD_DIM])

    xn = _rms_rows(x_ref[...], g_ref[...], NORM_EPS).astype(BF16)
    base = Q_LORA_RANK + KV_LORA_RANK
    down = jnp.dot(xn, win_ref[:, :base], preferred_element_type=F32)
    c_q = down[:, :Q_LORA_RANK]
    c_kv = down[:, Q_LORA_RANK:]
    kp = jnp.dot(xn, wkpe_ref[...], preferred_element_type=F32)
    kpe = (kp[:, :LANES] * cos_r_ref[...] + kp[:, LANES:] * sin_r_ref[...]).astype(k_ref.dtype)

    cqn = _rms_rows(c_q, qg_ref[...], NORM_EPS).astype(BF16)
    ckvn = _rms_rows(c_kv, kvg_ref[...], NORM_EPS).astype(BF16)

    q = lax.dot_general(wq_ref[...], cqn, _TT, preferred_element_type=F32)
    half = QK_ROPE_DIM // 2
    cos2 = jnp.concatenate([cos_t_ref[...]] * 2, axis=0)
    sin2 = jnp.concatenate([sin_t_ref[...]] * 2, axis=0)
    zpad = jnp.zeros((QK_PAD_DIM - QK_HEAD_DIM, q.shape[1]), q_ref.dtype)
    for h in range(N_HEADS):
        src, dst = h * QK_HEAD_DIM, h * QK_PAD_DIM
        q_ref[dst:dst + QK_NOPE_DIM, :] = (q[src:src + QK_NOPE_DIM] * q_scale).astype(q_ref.dtype)
        pe = q[src + QK_NOPE_DIM:src + QK_HEAD_DIM]
        rot = pe * cos2 + jnp.concatenate([-pe[half:], pe[:half]], axis=0) * sin2
        q_ref[dst + QK_NOPE_DIM:dst + QK_HEAD_DIM, :] = (rot * q_scale).astype(q_ref.dtype)
        q_ref[dst + QK_HEAD_DIM:dst + QK_PAD_DIM, :] = zpad

    kn = jnp.dot(ckvn, wkn_ref[...], preferred_element_type=F32).astype(k_ref.dtype)
    for h in range(N_HEADS):
        dst = h * QK_PAD_DIM
        k_ref[:, dst:dst + QK_NOPE_DIM] = kn[:, h * QK_NOPE_DIM:(h + 1) * QK_NOPE_DIM]
        k_ref[:, dst + QK_NOPE_DIM:dst + QK_PAD_DIM] = kpe
    v = lax.dot_general(wv_ref[...], ckvn, _TT, preferred_element_type=F32).astype(v_ref.dtype)
    ones = jnp.ones((V_PAD_DIM - V_HEAD_DIM, v.shape[1]), v_ref.dtype)
    for h in range(N_HEADS):
        v_ref[h * V_PAD_DIM:h * V_PAD_DIM + V_HEAD_DIM, :] = v[h * V_HEAD_DIM:(h + 1) * V_HEAD_DIM]
        v_ref[h * V_PAD_DIM + V_HEAD_DIM:(h + 1) * V_PAD_DIM, :] = ones


def _mla_proj(x, g, tables, weights, layer, q_g, kv_g, *, q_scale, tm=512):
    m, d = x.shape
    cos_t, sin_t, cos_r, sin_r = tables

    def resident(a):
        return pl.BlockSpec(a.shape, lambda i: (0,) * a.ndim)

    def layer_resident(a):
        return pl.BlockSpec((None,) + a.shape[1:], lambda i: (layer,) + (0,) * (a.ndim - 1))

    def rows(width):
        return pl.BlockSpec((tm, width), lambda i: (i, 0))

    def cols(height):
        return pl.BlockSpec((height, tm), lambda i: (0, i))

    half = QK_ROPE_DIM // 2
    return pl.pallas_call(
        functools.partial(_mla_proj_kernel, q_scale=q_scale),
        grid=(m // tm,),
        in_specs=[rows(d), resident(g), cols(half), cols(half), rows(LANES), rows(LANES),
                  layer_resident(weights[0]), resident(q_g), resident(kv_g)]
                 + [layer_resident(w) for w in weights[1:]],
        out_specs=[cols(N_HEADS * QK_PAD_DIM), rows(N_HEADS * QK_PAD_DIM),
                   pl.BlockSpec((None, N_HEADS * V_PAD_DIM, tm), lambda i: (i, 0, 0))],
        out_shape=[jax.ShapeDtypeStruct((N_HEADS * QK_PAD_DIM, m), BF16),
                   jax.ShapeDtypeStruct((m, N_HEADS * QK_PAD_DIM), BF16),
                   jax.ShapeDtypeStruct((m // tm, N_HEADS * V_PAD_DIM, tm), BF16)],
        scratch_shapes=[pltpu.VMEM((KV_LORA_RANK, N_HEADS * QK_NOPE_DIM), BF16),
                        pltpu.VMEM((KV_LORA_RANK, N_HEADS * V_HEAD_DIM), BF16)],
        compiler_params=pltpu.CompilerParams(
            dimension_semantics=("arbitrary",),
            vmem_limit_bytes=BIG_VMEM_LIMIT),
        name="mla_proj",
    )(x, g, cos_t, sin_t, cos_r, sin_r, weights[0], q_g, kv_g, *weights[1:])


def _attn_kernel(*refs, tq, tk, tv, hb, n_casts):
    q_ref, qn_ref, k_ref, v_ref = refs[:4]
    cast_in = refs[4:4 + n_casts]
    o_ref = refs[4 + n_casts]
    cast_out = refs[5 + n_casts:5 + 2 * n_casts]
    s_scr, m_scr, acc_scr = refs[-3:]
    _emit_casts(cast_in, cast_out)
    qi = pl.program_id(2)
    nv = tk // tv
    m_scr[...] = jnp.full(m_scr.shape, -jnp.inf, F32)
    acc_scr[...] = jnp.zeros(acc_scr.shape, F32)

    def scores(kb, c):
        r = pl.multiple_of(kb * tk, tk)
        return jnp.dot(k_ref[pl.ds(r, tk), c * QK_PAD_DIM:(c + 1) * QK_PAD_DIM],
                       q_ref[c * QK_PAD_DIM:(c + 1) * QK_PAD_DIM, :], preferred_element_type=F32)

    def update(kb, c, s):
        m_prev = m_scr[c]
        m_new = jnp.maximum(m_prev, jnp.max(s, axis=0, keepdims=True))
        alpha = jnp.exp2(m_prev - m_new)
        p = jnp.exp2(s - m_new).astype(BF16)
        m_scr[c] = m_new
        vs = slice(c * V_PAD_DIM, (c + 1) * V_PAD_DIM)
        pv = jnp.dot(v_ref[kb * nv, vs, :], p[:tv], preferred_element_type=F32)
        for t in range(1, nv):
            pv = pv + jnp.dot(v_ref[kb * nv + t, vs, :], p[t * tv:(t + 1) * tv],
                              preferred_element_type=F32)
        acc_scr[c] = alpha * acc_scr[c] + pv

    def advance(kb, cur, nxt):
        for c in range(hb):
            s_scr[nxt, c] = scores(kb + 1, c)
            update(kb, c, s_scr[cur, c])

    @pl.when(qi == 0)
    def _():
        for c in range(hb):
            s_scr[0, c] = scores(0, c)

    def body(t, carry):
        advance(2 * t, 0, 1)
        advance(2 * t + 1, 1, 0)
        return carry

    lax.fori_loop(0, qi // 2, body, 0)

    @pl.when(qi % 2 == 1)
    def _():
        advance(qi - 1, 0, 1)

    k_chunk = lax.broadcasted_iota(jnp.int32, (tk, tq), 0) // CHUNK
    q_chunk = lax.broadcasted_iota(jnp.int32, (tk, tq), 1) // CHUNK
    visible = k_chunk <= q_chunk

    def finish(slot):
        for c in range(hb):
            update(qi, c, jnp.where(visible, s_scr[slot, c], -jnp.inf))
            s_scr[0, c] = jnp.dot(k_ref[0:tk, c * QK_PAD_DIM:(c + 1) * QK_PAD_DIM],
                                  qn_ref[c * QK_PAD_DIM:(c + 1) * QK_PAD_DIM, :],
                                  preferred_element_type=F32)
            acc = acc_scr[c]
            o_ref[:, c * V_HEAD_DIM:(c + 1) * V_HEAD_DIM] = (
                acc[:V_HEAD_DIM] / acc[V_HEAD_DIM:V_HEAD_DIM + 1]).T.astype(o_ref.dtype)

    for parity in range(2):
        pl.when(qi % 2 == parity)(functools.partial(finish, parity))


def _attention(q_t, k, v_t, casts=(), *, batch, seq, tq=512, hb=4):
    tk = tq
    nq = seq // tq
    ng = N_HEADS // hb
    tv = v_t.shape[2]
    grid = (batch, ng, nq)
    c_in, c_out, c_shapes, c_args = _cast_specs(
        casts, batch * ng * nq, lambda b, h, i: (b * ng + h) * nq + i)
    outs = pl.pallas_call(
        functools.partial(_attn_kernel, tq=tq, tk=tk, tv=tv, hb=hb, n_casts=len(casts)),
        grid=grid,
        in_specs=[
            pl.BlockSpec((hb * QK_PAD_DIM, tq), lambda b, h, i: (h, b * nq + i)),
            pl.BlockSpec((hb * QK_PAD_DIM, tq), lambda b, h, i: (h, b * nq + jnp.minimum(i + 1, nq - 1))),
            pl.BlockSpec((seq, hb * QK_PAD_DIM), lambda b, h, i: (b, h)),
            pl.BlockSpec((seq // tv, hb * V_PAD_DIM, tv), lambda b, h, i: (b, h, 0)),
        ] + c_in,
        out_specs=[pl.BlockSpec((tq, hb * V_HEAD_DIM), lambda b, h, i: (b * nq + i, h))] + c_out,
        out_shape=[jax.ShapeDtypeStruct((batch * seq, N_HEADS * V_HEAD_DIM), BF16)] + c_shapes,
        scratch_shapes=[pltpu.VMEM((2, hb, tk, tq), F32), pltpu.VMEM((hb, 1, tq), F32),
                        pltpu.VMEM((hb, V_PAD_DIM, tq), F32)],
        compiler_params=pltpu.CompilerParams(
            dimension_semantics=("parallel", "parallel", "arbitrary"),
            vmem_limit_bytes=VMEM_LIMIT),
        name="mla_attention",
    )(q_t, q_t, k, v_t, *c_args)
    return outs[0], tuple(outs[1:])


def _prep_mla_weights(w_in, w_q_up, w_kv_up):
    w_in, w_q_up, w_kv_up = (w.astype(BF16) for w in (w_in, w_q_up, w_kv_up))
    nl, d, _ = w_in.shape
    half = QK_ROPE_DIM // 2
    base = Q_LORA_RANK + KV_LORA_RANK
    pad = jnp.zeros((nl, d, LANES - QK_ROPE_DIM), w_in.dtype)
    k_pe = w_in[:, :, base:]
    w_kpe = jnp.concatenate([k_pe, pad, -k_pe[:, :, half:], k_pe[:, :, :half], pad], axis=2)
    return w_in, w_kpe, w_q_up, w_kv_up


def kernel(x, positions, norm_mixer_g, norm_mlp_g, conv_w_pw1, conv_b_pw1, conv_w_dw, conv_b_dw, conv_ln_g, conv_ln_b, conv_w_pw2, conv_b_pw2, mla_w_in, mla_q_norm_g, mla_kv_norm_g, mla_w_q_up, mla_w_kv_up, mla_w_o, mlp_w1, mlp_w2, final_norm_g):
    batch, seq, d = x.shape
    m = batch * seq
    xs = x.reshape(m, d)

    def row(v):
        return v.reshape(1, -1)

    mla_weights = _prep_mla_weights(mla_w_in, mla_w_q_up, mla_w_kv_up)
    tables = _rope_tables(positions)
    q_scale = QK_HEAD_DIM ** -0.5 * math.log2(math.e)

    w_pw1 = conv_w_pw1[0].astype(BF16)
    w_o = None
    for layer in range(DEPTH):
        j = layer // N_MIXERS
        g_mix = row(norm_mixer_g[layer])
        if layer % N_MIXERS == 0:
            u = _pw1_glu(xs, g_mix, w_pw1, row(conv_b_pw1[j]))
            a, (w1, w2, w_pw2, w_o) = _dwconv_ln(
                u, conv_w_dw[j], row(conv_b_dw[j]), row(conv_ln_g[j]), row(conv_ln_b[j]),
                ((mlp_w1, layer), (mlp_w2, layer), (conv_w_pw2, j), (mla_w_o, j)), seq=seq)
            xs = _linear_residual(xs, a, w_pw2, row(conv_b_pw2[j]))
        else:
            q, k, v = _mla_proj(xs, g_mix, tables, mla_weights, j, row(mla_q_norm_g[j]),
                                row(mla_kv_norm_g[j]), q_scale=q_scale)
            casts = [(mlp_w1, layer), (mlp_w2, layer)]
            if j + 1 < conv_w_pw1.shape[0]:
                casts.append((conv_w_pw1, j + 1))
            o, (w1, w2, *rest) = _attention(q, k, v, tuple(casts), batch=batch, seq=seq)
            w_pw1 = rest[0] if rest else None
            xs = _linear_residual(xs, o, w_o)
        final_g = row(final_norm_g) if layer == DEPTH - 1 else None
        xs = _mlp(xs, row(norm_mlp_g[layer]), w1, w2, final_g)
    return xs.reshape(batch, seq, d)
```

```python
import functools
import math

import jax
import jax.numpy as jnp
from jax import lax
from jax.experimental import pallas as pl
from jax.experimental.pallas import tpu as pltpu

D_MODEL = 2048
DEPTH = 4
CHUNK = 64
N_MIXERS = 2
CONV_WIDTH = 31
N_HEADS = 16
QK_NOPE_DIM = 128
QK_ROPE_DIM = 64
V_HEAD_DIM = 128
Q_LORA_RANK = 512
KV_LORA_RANK = 512
D_FF = 4 * D_MODEL
ROPE_THETA = 10000.0
NORM_EPS = 1e-6
LN_EPS = 1e-5

SUBLANES = 8
LANES = 128
QK_HEAD_DIM = QK_NOPE_DIM + QK_ROPE_DIM
V_PAD_DIM = V_HEAD_DIM + 16
QK_PAD_DIM = 2 * LANES
HALO = 32
CONV_ROWS = 64
CONV_COLS = 512
VMEM_LIMIT = 56 * 1024 * 1024
BIG_VMEM_LIMIT = 60 * 1024 * 1024

F32 = jnp.float32
BF16 = jnp.bfloat16

_TT = (((0,), (1,)), ((), ()))


def _rms_rows(x, g, eps):
    ms = jnp.mean(x * x, axis=-1, keepdims=True)
    return x * lax.rsqrt(ms + eps) * g


def _rmsnorm_into(x_ref, g_ref, out_ref, rows, chunk=256):
    g = g_ref[...]

    def body(c, carry):
        r = pl.multiple_of(c * chunk, chunk)
        x = x_ref[pl.ds(r, chunk), :]
        out_ref[pl.ds(r, chunk), :] = _rms_rows(x, g, NORM_EPS).astype(out_ref.dtype)
        return carry

    lax.fori_loop(0, rows // chunk, body, 0)


def _cast_specs(casts, n_steps, step_of):
    in_specs, out_specs, out_shapes, args = [], [], [], []
    for w_all, layer in casts:
        _, r, c = w_all.shape
        blk = (r // n_steps, c)
        in_specs.append(pl.BlockSpec((None,) + blk, lambda *ids, layer=layer: (layer, step_of(*ids), 0)))
        out_specs.append(pl.BlockSpec(blk, lambda *ids: (step_of(*ids), 0)))
        out_shapes.append(jax.ShapeDtypeStruct((r, c), BF16))
        args.append(w_all)
    return in_specs, out_specs, out_shapes, args


def _emit_casts(cast_in, cast_out):
    for src, dst in zip(cast_in, cast_out):
        dst[...] = src[...].astype(dst.dtype)


def _pw1_glu_kernel(x_ref, g_ref, wa_ref, wg_ref, ba_ref, bg_ref, o_ref, xn_ref, *, tm):
    @pl.when(pl.program_id(1) == 0)
    def _():
        _rmsnorm_into(x_ref, g_ref, xn_ref, tm)

    xn = xn_ref[...]
    tn = o_ref.shape[1]
    for cs in (slice(0, tn // 2), slice(tn // 2, tn)):
        a = jnp.dot(xn, wa_ref[:, cs], preferred_element_type=F32) + ba_ref[:, cs]
        gate = jnp.dot(xn, wg_ref[:, cs], preferred_element_type=F32) + bg_ref[:, cs]
        o_ref[:, cs] = a * jax.nn.sigmoid(gate)


def _pw1_glu(x, g, w, b, *, tm=1024, tn=1024):
    m, d = x.shape
    nj = d // tn
    return pl.pallas_call(
        functools.partial(_pw1_glu_kernel, tm=tm),
        grid=(m // tm, nj),
        in_specs=[
            pl.BlockSpec((tm, d), lambda i, j: (i, 0)),
            pl.BlockSpec((1, d), lambda i, j: (0, 0)),
            pl.BlockSpec((d, tn), lambda i, j: (0, j)),
            pl.BlockSpec((d, tn), lambda i, j: (0, j + nj)),
            pl.BlockSpec((1, tn), lambda i, j: (0, j)),
            pl.BlockSpec((1, tn), lambda i, j: (0, j + nj)),
        ],
        out_specs=pl.BlockSpec((tm, tn), lambda i, j: (i, j)),
        out_shape=jax.ShapeDtypeStruct((m, d), F32),
        scratch_shapes=[pltpu.VMEM((tm, d), BF16)],
        compiler_params=pltpu.CompilerParams(
            dimension_semantics=("parallel", "arbitrary"),
            vmem_limit_bytes=VMEM_LIMIT),
        name="pw1_glu",
    )(x, g, w, w, b, b)


def _dwconv_ln_kernel(*refs, tc, tiles_per_seq, n_casts):
    u_ref, halo_ref, w_ref, b_ref, lg_ref, lb_ref = refs[:6]
    cast_in = refs[6:6 + n_casts]
    o_ref = refs[6 + n_casts]
    cast_out = refs[7 + n_casts:7 + 2 * n_casts]
    sh_ref, conv_ref = refs[-2:]
    _emit_casts(cast_in, cast_out)
    d = u_ref.shape[1]
    first = (pl.program_id(0) % tiles_per_seq) == 0
    halo = halo_ref[...]
    sh_ref[0, 0:HALO, :] = jnp.where(first, jnp.zeros_like(halo), halo)
    sh_ref[0, HALO:, :] = u_ref[...]
    n = tc + HALO - SUBLANES
    for s in range(1, SUBLANES):
        sh_ref[s, 0:n, :] = sh_ref[0, s:s + n, :]

    off = HALO - (CONV_WIDTH - 1)

    def row_body(ri, carry):
        r = pl.multiple_of(ri * CONV_ROWS, CONV_ROWS)
        groups = range(0, CONV_ROWS, SUBLANES)
        for c in range(0, d, CONV_COLS):
            cs = slice(c, c + CONV_COLS)
            bias = b_ref[:, cs]
            accs = [bias for _ in groups]
            for k in range(CONV_WIDTH):
                a, s = divmod(k + off, SUBLANES)
                wk = w_ref[k, :, cs]
                accs = [acc + sh_ref[s, pl.ds(r + SUBLANES * a + g, SUBLANES), cs] * wk
                        for acc, g in zip(accs, groups)]
            for acc, g in zip(accs, groups):
                conv_ref[pl.ds(r + g, SUBLANES), cs] = acc
        return carry

    lax.fori_loop(0, tc // CONV_ROWS, row_body, 0)

    y = conv_ref[...]
    mu = jnp.mean(y, axis=-1, keepdims=True)
    yc = y - mu
    var = jnp.mean(yc * yc, axis=-1, keepdims=True)
    z = yc * lax.rsqrt(var + LN_EPS) * lg_ref[...] + lb_ref[...]
    o_ref[...] = (z * jax.nn.sigmoid(z)).astype(o_ref.dtype)


def _dwconv_ln(u, w_dw, b_dw, ln_g, ln_b, casts=(), *, seq, tc=256):
    m, d = u.shape
    halo_blocks = tc // HALO
    grid = (m // tc,)
    c_in, c_out, c_shapes, c_args = _cast_specs(casts, grid[0], lambda i: i)
    w_rep = jnp.broadcast_to(w_dw[:, None, :], (CONV_WIDTH, SUBLANES, d))
    b_rep = jnp.broadcast_to(b_dw, (SUBLANES, d))
    outs = pl.pallas_call(
        functools.partial(_dwconv_ln_kernel, tc=tc, tiles_per_seq=seq // tc, n_casts=len(casts)),
        grid=grid,
        in_specs=[
            pl.BlockSpec((tc, d), lambda i: (i, 0)),
            pl.BlockSpec((HALO, d), lambda i: (jnp.maximum(i * halo_blocks - 1, 0), 0)),
            pl.BlockSpec((CONV_WIDTH, SUBLANES, d), lambda i: (0, 0, 0)),
            pl.BlockSpec((SUBLANES, d), lambda i: (0, 0)),
            pl.BlockSpec((1, d), lambda i: (0, 0)),
            pl.BlockSpec((1, d), lambda i: (0, 0)),
        ] + c_in,
        out_specs=[pl.BlockSpec((tc, d), lambda i: (i, 0))] + c_out,
        out_shape=[jax.ShapeDtypeStruct((m, d), BF16)] + c_shapes,
        scratch_shapes=[pltpu.VMEM((SUBLANES, tc + HALO, d), F32), pltpu.VMEM((tc, d), F32)],
        compiler_params=pltpu.CompilerParams(
            dimension_semantics=("parallel",),
            vmem_limit_bytes=VMEM_LIMIT),
        name="dwconv_ln",
    )(u, u, w_rep, b_rep, ln_g, ln_b, *c_args)
    return outs[0], tuple(outs[1:])


def _linear_residual_kernel(*refs, has_bias):
    if has_bias:
        x_ref, a_ref, w_ref, b_ref, o_ref = refs
    else:
        x_ref, a_ref, w_ref, o_ref = refs
    y = jnp.dot(a_ref[...], w_ref[...], preferred_element_type=F32)
    if has_bias:
        y = y + b_ref[...]
    o_ref[...] = x_ref[...] + y


def _linear_residual(x, a, w, b=None, *, tm=1024, tn=D_MODEL):
    m, n = x.shape
    k = a.shape[1]
    w_mode = pl.Buffered(1) if tn == n else None
    in_specs = [
        pl.BlockSpec((tm, tn), lambda i, j: (i, j)),
        pl.BlockSpec((tm, k), lambda i, j: (i, 0)),
        pl.BlockSpec((k, tn), lambda i, j: (0, j), pipeline_mode=w_mode),
    ]
    args = [x, a, w]
    if b is not None:
        in_specs.append(pl.BlockSpec((1, tn), lambda i, j: (0, j)))
        args.append(b)
    return pl.pallas_call(
        functools.partial(_linear_residual_kernel, has_bias=b is not None),
        grid=(m // tm, n // tn),
        in_specs=in_specs,
        out_specs=pl.BlockSpec((tm, tn), lambda i, j: (i, j)),
        out_shape=jax.ShapeDtypeStruct((m, n), F32),
        compiler_params=pltpu.CompilerParams(
            dimension_semantics=("parallel", "arbitrary"),
            vmem_limit_bytes=BIG_VMEM_LIMIT),
        name="linear_residual",
    )(*args)


def _mlp_kernel(*refs, tm, final_norm):
    if final_norm:
        x_ref, g_ref, w1_ref, w2_ref, fg_ref, o_ref, xn_ref = refs
    else:
        x_ref, g_ref, w1_ref, w2_ref, o_ref, xn_ref = refs
    f = pl.program_id(1)

    @pl.when(f == 0)
    def _():
        _rmsnorm_into(x_ref, g_ref, xn_ref, tm)
        o_ref[...] = x_ref[...]

    h = jnp.dot(xn_ref[...], w1_ref[...], preferred_element_type=F32)
    h = jnp.square(jnp.maximum(h, 0.0)).astype(BF16)
    o_ref[...] += jnp.dot(h, w2_ref[...], preferred_element_type=F32)

    if final_norm:
        @pl.when(f == pl.num_programs(1) - 1)
        def _():
            _rmsnorm_into(o_ref, fg_ref, o_ref, tm)


def _mlp(x, g, w1, w2, final_g=None, *, tm=1024, tf=1024):
    m, d = x.shape
    dff = w1.shape[1]
    final_norm = final_g is not None
    in_specs = [
        pl.BlockSpec((tm, d), lambda i, f: (i, 0)),
        pl.BlockSpec((1, d), lambda i, f: (0, 0)),
        pl.BlockSpec((d, tf), lambda i, f: (0, f)),
        pl.BlockSpec((tf, d), lambda i, f: (f, 0)),
    ]
    args = [x, g, w1, w2]
    if final_norm:
        in_specs.append(pl.BlockSpec((1, d), lambda i, f: (0, 0)))
        args.append(final_g)
    return pl.pallas_call(
        functools.partial(_mlp_kernel, tm=tm, final_norm=final_norm),
        grid=(m // tm, dff // tf),
        in_specs=in_specs,
        out_specs=pl.BlockSpec((tm, d), lambda i, f: (i, 0)),
        out_shape=jax.ShapeDtypeStruct((m, d), F32),
        scratch_shapes=[pltpu.VMEM((tm, d), BF16)],
        compiler_params=pltpu.CompilerParams(
            dimension_semantics=("parallel", "arbitrary"),
            vmem_limit_bytes=BIG_VMEM_LIMIT),
        name="mlp",
    )(*args)


def _rope_table_kernel(pos_ref, invf_ref, cos_t_ref, sin_t_ref, cos_r_ref, sin_r_ref):
    ang = pos_ref[...].astype(F32) * invf_ref[...]
    c = jnp.cos(ang)
    s = jnp.sin(ang)
    cos_t_ref[...] = c
    sin_t_ref[...] = s
    z = jnp.zeros((LANES - QK_ROPE_DIM, ang.shape[1]), F32)
    cos_r_ref[...] = jnp.concatenate([c, c, z], axis=0).T
    sin_r_ref[...] = jnp.concatenate([s, s, z], axis=0).T


def _rope_tables(positions, *, tr=2048):
    half = QK_ROPE_DIM // 2
    m = positions.size
    inv_freq = ROPE_THETA ** (-jnp.arange(0, QK_ROPE_DIM, 2, dtype=F32) / QK_ROPE_DIM)
    return pl.pallas_call(
        _rope_table_kernel,
        grid=(m // tr,),
        in_specs=[pl.BlockSpec((1, tr), lambda i: (0, i)),
                  pl.BlockSpec((half, 1), lambda i: (0, 0))],
        out_specs=[pl.BlockSpec((half, tr), lambda i: (0, i)),
                   pl.BlockSpec((half, tr), lambda i: (0, i)),
                   pl.BlockSpec((tr, LANES), lambda i: (i, 0)),
                   pl.BlockSpec((tr, LANES), lambda i: (i, 0))],
        out_shape=[jax.ShapeDtypeStruct((half, m), F32)] * 2
                  + [jax.ShapeDtypeStruct((m, LANES), F32)] * 2,
        compiler_params=pltpu.CompilerParams(dimension_semantics=("parallel",)),
        name="rope_tables",
    )(positions.reshape(1, m), inv_freq.reshape(half, 1))


def _mla_proj_kernel(x_ref, g_ref, cos_t_ref, sin_t_ref, cos_r_ref, sin_r_ref, win_ref, qg_ref, kvg_ref,
                     wkpe_ref, wq_ref, wkv_ref, q_ref, k_ref, v_ref, wkn_ref, wv_ref, *, q_scale):
    @pl.when(pl.program_id(0) == 0)
    def _():
        for h in range(N_HEADS):
            src = h * (QK_NOPE_DIM + V_HEAD_DIM)
            wkn_ref[:, h * QK_NOPE_DIM:(h + 1) * QK_NOPE_DIM] = wkv_ref[:, src:src + QK_NOPE_DIM]
            wv_ref[:, h * V_HEAD_DIM:(h + 1) * V_HEAD_DIM] = (
                wkv_ref[:, src + QK_NOPE_DIM:src + QK_NOPE_DIM + V_HEAD_DIM])

    xn = _rms_rows(x_ref[...], g_ref[...], NORM_EPS).astype(BF16)
    base = Q_LORA_RANK + KV_LORA_RANK
    down = jnp.dot(xn, win_ref[:, :base], preferred_element_type=F32)
    c_q = down[:, :Q_LORA_RANK]
    c_kv = down[:, Q_LORA_RANK:]
    kp = jnp.dot(xn, wkpe_ref[...], preferred_element_type=F32)
    kpe = (kp[:, :LANES] * cos_r_ref[...] + kp[:, LANES:] * sin_r_ref[...]).astype(k_ref.dtype)

    cqn = _rms_rows(c_q, qg_ref[...], NORM_EPS).astype(BF16)
    ckvn = _rms_rows(c_kv, kvg_ref[...], NORM_EPS).astype(BF16)

    q = lax.dot_general(wq_ref[...], cqn, _TT, preferred_element_type=F32)
    half = QK_ROPE_DIM // 2
    cos2 = jnp.concatenate([cos_t_ref[...]] * 2, axis=0)
    sin2 = jnp.concatenate([sin_t_ref[...]] * 2, axis=0)
    zpad = jnp.zeros((QK_PAD_DIM - QK_HEAD_DIM, q.shape[1]), q_ref.dtype)
    for h in range(N_HEADS):
        src, dst = h * QK_HEAD_DIM, h * QK_PAD_DIM
        q_ref[dst:dst + QK_NOPE_DIM, :] = (q[src:src + QK_NOPE_DIM] * q_scale).astype(q_ref.dtype)
        pe = q[src + QK_NOPE_DIM:src + QK_HEAD_DIM]
        rot = pe * cos2 + jnp.concatenate([-pe[half:], pe[:half]], axis=0) * sin2
        q_ref[dst + QK_NOPE_DIM:dst + QK_HEAD_DIM, :] = (rot * q_scale).astype(q_ref.dtype)
        q_ref[dst + QK_HEAD_DIM:dst + QK_PAD_DIM, :] = zpad

    kn = jnp.dot(ckvn, wkn_ref[...], preferred_element_type=F32).astype(k_ref.dtype)
    for h in range(N_HEADS):
        dst = h * QK_PAD_DIM
        k_ref[:, dst:dst + QK_NOPE_DIM] = kn[:, h * QK_NOPE_DIM:(h + 1) * QK_NOPE_DIM]
        k_ref[:, dst + QK_NOPE_DIM:dst + QK_PAD_DIM] = kpe
    v = lax.dot_general(wv_ref[...], ckvn, _TT, preferred_element_type=F32).astype(v_ref.dtype)
    ones = jnp.ones((V_PAD_DIM - V_HEAD_DIM, v.shape[1]), v_ref.dtype)
    for h in range(N_HEADS):
        v_ref[h * V_PAD_DIM:h * V_PAD_DIM + V_HEAD_DIM, :] = v[h * V_HEAD_DIM:(h + 1) * V_HEAD_DIM]
        v_ref[h * V_PAD_DIM + V_HEAD_DIM:(h + 1) * V_PAD_DIM, :] = ones


def _mla_proj(x, g, tables, weights, layer, q_g, kv_g, *, q_scale, tm=512):
    m, d = x.shape
    cos_t, sin_t, cos_r, sin_r = tables

    def resident(a):
        return pl.BlockSpec(a.shape, lambda i: (0,) * a.ndim)

    def layer_resident(a):
        return pl.BlockSpec((None,) + a.shape[1:], lambda i: (layer,) + (0,) * (a.ndim - 1))

    def rows(width):
        return pl.BlockSpec((tm, width), lambda i: (i, 0))

    def cols(height):
        return pl.BlockSpec((height, tm), lambda i: (0, i))

    half = QK_ROPE_DIM // 2
    return pl.pallas_call(
        functools.partial(_mla_proj_kernel, q_scale=q_scale),
        grid=(m // tm,),
        in_specs=[rows(d), resident(g), cols(half), cols(half), rows(LANES), rows(LANES),
                  layer_resident(weights[0]), resident(q_g), resident(kv_g)]
                 + [layer_resident(w) for w in weights[1:]],
        out_specs=[cols(N_HEADS * QK_PAD_DIM), rows(N_HEADS * QK_PAD_DIM),
                   pl.BlockSpec((None, N_HEADS * V_PAD_DIM, tm), lambda i: (i, 0, 0))],
        out_shape=[jax.ShapeDtypeStruct((N_HEADS * QK_PAD_DIM, m), BF16),
                   jax.ShapeDtypeStruct((m, N_HEADS * QK_PAD_DIM), BF16),
                   jax.ShapeDtypeStruct((m // tm, N_HEADS * V_PAD_DIM, tm), BF16)],
        scratch_shapes=[pltpu.VMEM((KV_LORA_RANK, N_HEADS * QK_NOPE_DIM), BF16),
                        pltpu.VMEM((KV_LORA_RANK, N_HEADS * V_HEAD_DIM), BF16)],
        compiler_params=pltpu.CompilerParams(
            dimension_semantics=("arbitrary",),
            vmem_limit_bytes=BIG_VMEM_LIMIT),
        name="mla_proj",
    )(x, g, cos_t, sin_t, cos_r, sin_r, weights[0], q_g, kv_g, *weights[1:])


def _attn_kernel(*refs, tq, tk, tv, hb, n_casts):
    q_ref, qn_ref, k_ref, v_ref = refs[:4]
    cast_in = refs[4:4 + n_casts]
    o_ref = refs[4 + n_casts]
    cast_out = refs[5 + n_casts:5 + 2 * n_casts]
    s_scr, m_scr, acc_scr = refs[-3:]
    _emit_casts(cast_in, cast_out)
    qi = pl.program_id(2)
    nv = tk // tv
    m_scr[...] = jnp.full(m_scr.shape, -jnp.inf, F32)
    acc_scr[...] = jnp.zeros(acc_scr.shape, F32)

    def scores(kb, c):
        r = pl.multiple_of(kb * tk, tk)
        return jnp.dot(k_ref[pl.ds(r, tk), c * QK_PAD_DIM:(c + 1) * QK_PAD_DIM],
                       q_ref[c * QK_PAD_DIM:(c + 1) * QK_PAD_DIM, :], preferred_element_type=F32)

    def update(kb, c, s):
        m_prev = m_scr[c]
        m_new = jnp.maximum(m_prev, jnp.max(s, axis=0, keepdims=True))
        alpha = jnp.exp2(m_prev - m_new)
        p = jnp.exp2(s - m_new).astype(BF16)
        m_scr[c] = m_new
        vs = slice(c * V_PAD_DIM, (c + 1) * V_PAD_DIM)
        pv = jnp.dot(v_ref[kb * nv, vs, :], p[:tv], preferred_element_type=F32)
        for t in range(1, nv):
            pv = pv + jnp.dot(v_ref[kb * nv + t, vs, :], p[t * tv:(t + 1) * tv],
                              preferred_element_type=F32)
        acc_scr[c] = alpha * acc_scr[c] + pv

    def advance(kb, cur, nxt):
        for c in range(hb):
            s_scr[nxt, c] = scores(kb + 1, c)
            update(kb, c, s_scr[cur, c])

    @pl.when(qi == 0)
    def _():
        for c in range(hb):
            s_scr[0, c] = scores(0, c)

    def body(t, carry):
        advance(2 * t, 0, 1)
        advance(2 * t + 1, 1, 0)
        return carry

    lax.fori_loop(0, qi // 2, body, 0)

    @pl.when(qi % 2 == 1)
    def _():
        advance(qi - 1, 0, 1)

    k_chunk = lax.broadcasted_iota(jnp.int32, (tk, tq), 0) // CHUNK
    q_chunk = lax.broadcasted_iota(jnp.int32, (tk, tq), 1) // CHUNK
    visible = k_chunk <= q_chunk

    def finish(slot):
        for c in range(hb):
            update(qi, c, jnp.where(visible, s_scr[slot, c], -jnp.inf))
            s_scr[0, c] = jnp.dot(k_ref[0:tk, c * QK_PAD_DIM:(c + 1) * QK_PAD_DIM],
                                  qn_ref[c * QK_PAD_DIM:(c + 1) * QK_PAD_DIM, :],
                                  preferred_element_type=F32)
            acc = acc_scr[c]
            o_ref[:, c * V_HEAD_DIM:(c + 1) * V_HEAD_DIM] = (
                acc[:V_HEAD_DIM] / acc[V_HEAD_DIM:V_HEAD_DIM + 1]).T.astype(o_ref.dtype)

    for parity in range(2):
        pl.when(qi % 2 == parity)(functools.partial(finish, parity))


def _attention(q_t, k, v_t, casts=(), *, batch, seq, tq=512, hb=4):
    tk = tq
    nq = seq // tq
    ng = N_HEADS // hb
    tv = v_t.shape[2]
    grid = (batch, ng, nq)
    c_in, c_out, c_shapes, c_args = _cast_specs(
        casts, batch * ng * nq, lambda b, h, i: (b * ng + h) * nq + i)
    outs = pl.pallas_call(
        functools.partial(_attn_kernel, tq=tq, tk=tk, tv=tv, hb=hb, n_casts=len(casts)),
        grid=grid,
        in_specs=[
            pl.BlockSpec((hb * QK_PAD_DIM, tq), lambda b, h, i: (h, b * nq + i)),
            pl.BlockSpec((hb * QK_PAD_DIM, tq), lambda b, h, i: (h, b * nq + jnp.minimum(i + 1, nq - 1))),
            pl.BlockSpec((seq, hb * QK_PAD_DIM), lambda b, h, i: (b, h)),
            pl.BlockSpec((seq // tv, hb * V_PAD_DIM, tv), lambda b, h, i: (b, h, 0)),
        ] + c_in,
        out_specs=[pl.BlockSpec((tq, hb * V_HEAD_DIM), lambda b, h, i: (b * nq + i, h))] + c_out,
        out_shape=[jax.ShapeDtypeStruct((batch * seq, N_HEADS * V_HEAD_DIM), BF16)] + c_shapes,
        scratch_shapes=[pltpu.VMEM((2, hb, tk, tq), F32), pltpu.VMEM((hb, 1, tq), F32),
                        pltpu.VMEM((hb, V_PAD_DIM, tq), F32)],
        compiler_params=pltpu.CompilerParams(
            dimension_semantics=("parallel", "parallel", "arbitrary"),
            vmem_limit_bytes=VMEM_LIMIT),
        name="mla_attention",
    )(q_t, q_t, k, v_t, *c_args)
    return outs[0], tuple(outs[1:])


def _prep_mla_weights(w_in, w_q_up, w_kv_up):
    w_in, w_q_up, w_kv_up = (w.astype(BF16) for w in (w_in, w_q_up, w_kv_up))
    nl, d, _ = w_in.shape
    half = QK_ROPE_DIM // 2
    base = Q_LORA_RANK + KV_LORA_RANK
    pad = jnp.zeros((nl, d, LANES - QK_ROPE_DIM), w_in.dtype)
    k_pe = w_in[:, :, base:]
    w_kpe = jnp.concatenate([k_pe, pad, -k_pe[:, :, half:], k_pe[:, :, :half], pad], axis=2)
    return w_in, w_kpe, w_q_up, w_kv_up


def kernel(x, positions, norm_mixer_g, norm_mlp_g, conv_w_pw1, conv_b_pw1, conv_w_dw, conv_b_dw, conv_ln_g, conv_ln_b, conv_w_pw2, conv_b_pw2, mla_w_in, mla_q_norm_g, mla_kv_norm_g, mla_w_q_up, mla_w_kv_up, mla_w_o, mlp_w1, mlp_w2, final_norm_g):
    batch, seq, d = x.shape
    m = batch * seq
    xs = x.reshape(m, d)

    def row(v):
        return v.reshape(1, -1)

    mla_weights = _prep_mla_weights(mla_w_in, mla_w_q_up, mla_w_kv_up)
    tables = _rope_tables(positions)
    q_scale = QK_HEAD_DIM ** -0.5 * math.log2(math.e)

    w_pw1 = conv_w_pw1[0].astype(BF16)
    w_o = None
    for layer in range(DEPTH):
        j = layer // N_MIXERS
        g_mix = row(norm_mixer_g[layer])
        if layer % N_MIXERS == 0:
            u = _pw1_glu(xs, g_mix, w_pw1, row(conv_b_pw1[j]))
            a, (w1, w2, w_pw2, w_o) = _dwconv_ln(
                u, conv_w_dw[j], row(conv_b_dw[j]), row(conv_ln_g[j]), row(conv_ln_b[j]),
                ((mlp_w1, layer), (mlp_w2, layer), (conv_w_pw2, j), (mla_w_o, j)), seq=seq)
            xs = _linear_residual(xs, a, w_pw2, row(conv_b_pw2[j]))
        else:
            q, k, v = _mla_proj(xs, g_mix, tables, mla_weights, j, row(mla_q_norm_g[j]),
                                row(mla_kv_norm_g[j]), q_scale=q_scale)
            casts = [(mlp_w1, layer), (mlp_w2, layer)]
            if j + 1 < conv_w_pw1.shape[0]:
                casts.append((conv_w_pw1, j + 1))
            o, (w1, w2, *rest) = _attention(q, k, v, tuple(casts), batch=batch, seq=seq)
            w_pw1 = rest[0] if rest else None
            xs = _linear_residual(xs, o, w_o)
        final_g = row(final_norm_g) if layer == DEPTH - 1 else None
        xs = _mlp(xs, row(norm_mlp_g[layer]), w1, w2, final_g)
    return xs.reshape(batch, seq, d)
```
